```python
import math
import jax, jax.numpy as jnp
from jax import lax
import numpy as np

D_MODEL = 2048
BATCH = 1
SEQ = 16384
DEPTH = 4
DEC_BATCH = 16
DEC_SEQ = 16
PAST_LEN = 1024

CHUNK = 64
N_MIXERS = 3
N_A_LAYERS = (DEPTH + N_MIXERS - 1) // N_MIXERS
N_B_LAYERS = (DEPTH - 1 + N_MIXERS - 1) // N_MIXERS
N_C_LAYERS = (DEPTH - 2 + N_MIXERS - 1) // N_MIXERS

N_HEADS = 32
N_KV_HEADS = 4
HEAD_DIM = 64
GROUP = N_HEADS // N_KV_HEADS
QKV_DIM = (N_HEADS + 2 * N_KV_HEADS) * HEAD_DIM
WINDOW = 128
WIN_BLOCKS = WINDOW // CHUNK
ROT_DIM = HEAD_DIM // 4
ROPE_THETA = 500000.0

POOL_WINDOWS = (2, 4, 8, 16)
N_POOL_GROUPS = len(POOL_WINDOWS)
POOL_GROUP_DIM = D_MODEL // N_POOL_GROUPS
POOL_PREFIX = max(POOL_WINDOWS) - 1

CONV_WIDTH = 31
CONV_PREFIX = CONV_WIDTH - 1

D_FF = -(-8 * D_MODEL // (3 * 256)) * 256

RMS_EPS = 1e-5
LN_EPS = 1e-5

kernel_name = "chunk_causal_hybrid_swa_pool_conv_step"


def rms_norm(x, g):
    xf = x.astype(jnp.float32)
    y = xf * lax.rsqrt(jnp.mean(xf * xf, axis=-1, keepdims=True) + RMS_EPS)
    return (y * g.astype(jnp.float32)).astype(x.dtype)


def layer_norm(x, g, b):
    xf = x.astype(jnp.float32)
    mu = jnp.mean(xf, axis=-1, keepdims=True)
    var = jnp.mean(jnp.square(xf - mu), axis=-1, keepdims=True)
    y = (xf - mu) * lax.rsqrt(var + LN_EPS)
    return (y * g.astype(jnp.float32) + b.astype(jnp.float32)).astype(x.dtype)


def rope_partial(x, pos):
    inv_freq = ROPE_THETA ** (-jnp.arange(0, ROT_DIM, 2, dtype=jnp.float32) / ROT_DIM)
    ang = pos.astype(jnp.float32)[:, None] * inv_freq[None, :]
    cos = jnp.cos(ang)[None, :, None, :]
    sin = jnp.sin(ang)[None, :, None, :]
    xr = x[..., :ROT_DIM].astype(jnp.float32)
    x1, x2 = xr[..., :ROT_DIM // 2], xr[..., ROT_DIM // 2:]
    rot = jnp.concatenate([x1 * cos - x2 * sin, x2 * cos + x1 * sin], axis=-1).astype(x.dtype)
    return jnp.concatenate([rot, x[..., ROT_DIM:]], axis=-1)


def qkv_rope(h, w_qkv, b_qkv, pos):
    bsz, seq, _ = h.shape
    qkv = h @ w_qkv + b_qkv
    nq, nk = N_HEADS * HEAD_DIM, N_KV_HEADS * HEAD_DIM
    q = qkv[..., :nq].reshape(bsz, seq, N_HEADS, HEAD_DIM)
    k = qkv[..., nq:nq + nk].reshape(bsz, seq, N_KV_HEADS, HEAD_DIM)
    v = qkv[..., nq + nk:].reshape(bsz, seq, N_KV_HEADS, HEAD_DIM)
    return rope_partial(q, pos), rope_partial(k, pos), v


def sink_softmax(scores, sinks):
    sink = sinks.astype(jnp.float32).reshape(N_KV_HEADS, GROUP, 1, 1)
    m = jnp.maximum(jnp.max(scores, axis=-1, keepdims=True), sink)
    p = jnp.exp(scores - m)
    return p / (jnp.sum(p, axis=-1, keepdims=True) + jnp.exp(sink - m))


def attn_prompt(h, w_qkv, b_qkv, sinks, w_o, b_o):
    bsz, seq, _ = h.shape
    n_ch = seq // CHUNK
    q, k, v = qkv_rope(h, w_qkv, b_qkv, jnp.arange(seq))
    qb = q.reshape(bsz, n_ch, CHUNK, N_KV_HEADS, GROUP, HEAD_DIM)
    pad = ((0, 0), (WIN_BLOCKS, 0), (0, 0), (0, 0), (0, 0))
    kc = jnp.pad(k.reshape(bsz, n_ch, CHUNK, N_KV_HEADS, HEAD_DIM), pad)
    vc = jnp.pad(v.reshape(bsz, n_ch, CHUNK, N_KV_HEADS, HEAD_DIM), pad)
    kb = jnp.concatenate([kc[:, j:j + n_ch] for j in range(WIN_BLOCKS + 1)], axis=2)
    vb = jnp.concatenate([vc[:, j:j + n_ch] for j in range(WIN_BLOCKS + 1)], axis=2)
    scores = jnp.einsum('bnqhgd,bnshd->bnhgqs', qb, kb,
                        preferred_element_type=jnp.float32) * (HEAD_DIM ** -0.5)
    blk = jnp.arange((WIN_BLOCKS + 1) * CHUNK) // CHUNK
    valid = (jnp.arange(n_ch)[:, None] + blk[None, :]) >= WIN_BLOCKS
    scores = jnp.where(valid[None, :, None, None, None, :], scores, -jnp.inf)
    p = sink_softmax(scores, sinks)
    o = jnp.einsum('bnhgqs,bnshd->bnqhgd', p.astype(vb.dtype), vb)
    y = o.reshape(bsz, seq, N_HEADS * HEAD_DIM) @ w_o + b_o
    keep = min(WINDOW, seq)
    return y, k[:, seq - keep:], v[:, seq - keep:]


def attn_sample(h, cache_k, cache_v, w_qkv, b_qkv, sinks, w_o, b_o):
    bsz, seq, _ = h.shape
    w_cache = cache_k.shape[1]
    q, k, v = qkv_rope(h, w_qkv, b_qkv, PAST_LEN + jnp.arange(seq))
    kk = jnp.concatenate([cache_k, k], axis=1)
    vv = jnp.concatenate([cache_v, v], axis=1)
    qg = q.reshape(bsz, seq, N_KV_HEADS, GROUP, HEAD_DIM)
    scores = jnp.einsum('bqhgd,bshd->bhgqs', qg, kk,
                        preferred_element_type=jnp.float32) * (HEAD_DIM ** -0.5)
    p = sink_softmax(scores, sinks)
    o = jnp.einsum('bhgqs,bshd->bqhgd', p.astype(vv.dtype), vv)
    y = o.reshape(bsz, seq, N_HEADS * HEAD_DIM) @ w_o + b_o
    return y, kk[:, -w_cache:], vv[:, -w_cache:]


def pool_mix(h, prefix, pos0, w_group, scale):
    bsz, seq, _ = h.shape
    ext = jnp.concatenate([prefix, h], axis=1)
    cs = jnp.cumsum(jnp.pad(ext.astype(jnp.float32), ((0, 0), (1, 0), (0, 0))), axis=1)
    end = cs[:, POOL_PREFIX + 1:]
    pos = pos0 + jnp.arange(seq)
    outs = []
    for g, w in enumerate(POOL_WINDOWS):
        sl = slice(g * POOL_GROUP_DIM, (g + 1) * POOL_GROUP_DIM)
        start = cs[:, POOL_PREFIX + 1 - w:POOL_PREFIX + 1 - w + seq, sl]
        cnt = jnp.minimum(pos + 1, w).astype(jnp.float32)[None, :, None]
        outs.append((end[..., sl] - start) / cnt - h[..., sl].astype(jnp.float32))
    pooled = jnp.stack(outs, axis=2).astype(h.dtype)
    mixed = jnp.einsum('btgc,gcd->btgd', pooled, w_group).reshape(bsz, seq, D_MODEL)
    return mixed * scale, ext[:, -POOL_PREFIX:]


def conv_mix(h, prefix, w_pw1, b_pw1, w_dw, b_dw, ln_g, ln_b, w_pw2, b_pw2):
    a = h @ w_pw1 + b_pw1
    u = a[..., :D_MODEL] * jax.nn.sigmoid(a[..., D_MODEL:])
    ext = jnp.concatenate([prefix, u], axis=1)
    c = lax.conv_general_dilated(ext, w_dw[:, None, :].astype(ext.dtype), window_strides=(1,),
                                 padding='VALID', dimension_numbers=('NWC', 'WIO', 'NWC'),
                                 feature_group_count=D_MODEL) + b_dw
    z = jax.nn.silu(layer_norm(c, ln_g, ln_b))
    return z @ w_pw2 + b_pw2, ext[:, -CONV_PREFIX:]


def swiglu(h, w_gate_up, w_down):
    a = h @ w_gate_up
    return (jax.nn.silu(a[..., :D_FF]) * a[..., D_FF:]) @ w_down


def setup_inputs(seed: int = 0) -> dict:
    key = jax.random.key(seed)
    ks = jax.random.split(key, 32)
    f32 = jnp.float32

    def nrm(k, shape, scale):
        return jax.random.normal(k, shape, f32) * scale

    w_cache = min(WINDOW, PAST_LEN)
    return {
        'x_prompt': nrm(ks[0], (BATCH, SEQ, D_MODEL), 1.0),
        'x_sample': nrm(ks[1], (DEC_BATCH, DEC_SEQ, D_MODEL), 1.0),
        'cache_k': nrm(ks[2], (N_A_LAYERS, DEC_BATCH, w_cache, N_KV_HEADS, HEAD_DIM), 1.0),
        'cache_v': nrm(ks[3], (N_A_LAYERS, DEC_BATCH, w_cache, N_KV_HEADS, HEAD_DIM), 1.0),
        'state_pool': nrm(ks[4], (N_B_LAYERS, DEC_BATCH, POOL_PREFIX, D_MODEL), 1.0),
        'state_conv': nrm(ks[5], (N_C_LAYERS, DEC_BATCH, CONV_PREFIX, D_MODEL), 0.5),
        'norm_mix': 1.0 + nrm(ks[6], (DEPTH, D_MODEL), 0.02),
        'norm_ffn': 1.0 + nrm(ks[7], (DEPTH, D_MODEL), 0.02),
        'norm_final': 1.0 + nrm(ks[8], (D_MODEL,), 0.02),
        'a_w_qkv': nrm(ks[9], (N_A_LAYERS, D_MODEL, QKV_DIM), D_MODEL ** -0.5),
        'a_b_qkv': nrm(ks[10], (N_A_LAYERS, QKV_DIM), 0.02),
        'a_sinks': nrm(ks[11], (N_A_LAYERS, N_HEADS), 1.0),
        'a_w_o': nrm(ks[12], (N_A_LAYERS, N_HEADS * HEAD_DIM, D_MODEL), (N_HEADS * HEAD_DIM) ** -0.5),
        'a_b_o': nrm(ks[13], (N_A_LAYERS, D_MODEL), 0.02),
        'b_w_group': nrm(ks[14], (N_B_LAYERS, N_POOL_GROUPS, POOL_GROUP_DIM, POOL_GROUP_DIM), POOL_GROUP_DIM ** -0.5),
        'b_scale': 1.0 + nrm(ks[15], (N_B_LAYERS, D_MODEL), 0.02),
        'c_w_pw1': nrm(ks[16], (N_C_LAYERS, D_MODEL, 2 * D_MODEL), D_MODEL ** -0.5),
        'c_b_pw1': nrm(ks[17], (N_C_LAYERS, 2 * D_MODEL), 0.02),
        'c_w_dw': nrm(ks[18], (N_C_LAYERS, CONV_WIDTH, D_MODEL), CONV_WIDTH ** -0.5),
        'c_b_dw': nrm(ks[19], (N_C_LAYERS, D_MODEL), 0.02),
        'c_ln_g': 1.0 + nrm(ks[20], (N_C_LAYERS, D_MODEL), 0.02),
        'c_ln_b': nrm(ks[21], (N_C_LAYERS, D_MODEL), 0.02),
        'c_w_pw2': nrm(ks[22], (N_C_LAYERS, D_MODEL, D_MODEL), D_MODEL ** -0.5),
        'c_b_pw2': nrm(ks[23], (N_C_LAYERS, D_MODEL), 0.02),
        'f_w_gate_up': nrm(ks[24], (DEPTH, D_MODEL, 2 * D_FF), D_MODEL ** -0.5),
        'f_w_down': nrm(ks[25], (DEPTH, D_FF, D_MODEL), D_FF ** -0.5),
    }


def reference(x_prompt, x_sample, cache_k, cache_v, state_pool, state_conv,
              norm_mix, norm_ffn, norm_final,
              a_w_qkv, a_b_qkv, a_sinks, a_w_o, a_b_o,
              b_w_group, b_scale,
              c_w_pw1, c_b_pw1, c_w_dw, c_b_dw, c_ln_g, c_ln_b, c_w_pw2, c_b_pw2,
              f_w_gate_up, f_w_down):
    xp, xs = x_prompt, x_sample
    bp = x_prompt.shape[0]
    kp_l, vp_l, ks_l, vs_l = [], [], [], []
    pp_l, ps_l, cp_l, cs_l = [], [], [], []
    for i in range(DEPTH):
        kind, j = i % N_MIXERS, i // N_MIXERS
        hp = rms_norm(xp, norm_mix[i])
        hs = rms_norm(xs, norm_mix[i])
        if kind == 0:
            yp, nk, nv = attn_prompt(hp, a_w_qkv[j], a_b_qkv[j], a_sinks[j], a_w_o[j], a_b_o[j])
            kp_l.append(nk); vp_l.append(nv)
            ys, nk, nv = attn_sample(hs, cache_k[j], cache_v[j], a_w_qkv[j], a_b_qkv[j],
                                     a_sinks[j], a_w_o[j], a_b_o[j])
            ks_l.append(nk); vs_l.append(nv)
        elif kind == 1:
            zero_prefix = jnp.zeros((bp, POOL_PREFIX, D_MODEL), hp.dtype)
            yp, st = pool_mix(hp, zero_prefix, 0, b_w_group[j], b_scale[j])
            pp_l.append(st)
            ys, st = pool_mix(hs, state_pool[j].astype(hs.dtype), PAST_LEN, b_w_group[j], b_scale[j])
            ps_l.append(st)
        else:
            zero_prefix = jnp.zeros((bp, CONV_PREFIX, D_MODEL), hp.dtype)
            yp, st = conv_mix(hp, zero_prefix, c_w_pw1[j], c_b_pw1[j], c_w_dw[j], c_b_dw[j],
                              c_ln_g[j], c_ln_b[j], c_w_pw2[j], c_b_pw2[j])
            cp_l.append(st)
            ys, st = conv_mix(hs, state_conv[j].astype(hs.dtype), c_w_pw1[j], c_b_pw1[j], c_w_dw[j],
                              c_b_dw[j], c_ln_g[j], c_ln_b[j], c_w_pw2[j], c_b_pw2[j])
            cs_l.append(st)
        xp = xp + yp
        xs = xs + ys
        xp = xp + swiglu(rms_norm(xp, norm_ffn[i]), f_w_gate_up[i], f_w_down[i])
        xs = xs + swiglu(rms_norm(xs, norm_ffn[i]), f_w_gate_up[i], f_w_down[i])
    y_prompt = rms_norm(xp, norm_final)
    y_sample = rms_norm(xs, norm_final)
    return (y_prompt, y_sample,
            jnp.stack(kp_l), jnp.stack(vp_l), jnp.stack(ks_l), jnp.stack(vs_l),
            jnp.stack(pp_l), jnp.stack(ps_l), jnp.stack(cp_l), jnp.stack(cs_l))
```

```python
import functools
import math

import jax
import jax.numpy as jnp
from jax import lax
from jax.experimental import pallas as pl
from jax.experimental.pallas import tpu as pltpu

F32 = jnp.float32
BF16 = jnp.bfloat16

D_MODEL = 2048
DEPTH = 4
PAST_LEN = 1024
CHUNK = 64
N_MIXERS = 3

N_HEADS = 32
N_KV_HEADS = 4
HEAD_DIM = 64
GROUP = N_HEADS // N_KV_HEADS
Q_DIM = N_HEADS * HEAD_DIM
KV_DIM = N_KV_HEADS * HEAD_DIM
QKV_DIM = Q_DIM + 2 * KV_DIM
WINDOW = 128
ROT_DIM = HEAD_DIM // 4
ROPE_THETA = 500000.0

POOL_WINDOWS = (2, 4, 8, 16)
POOL_GROUP_DIM = D_MODEL // len(POOL_WINDOWS)
POOL_PREFIX = max(POOL_WINDOWS) - 1
POOL_PAD = 16

CONV_WIDTH = 31
CONV_PREFIX = CONV_WIDTH - 1
CONV_PAD = 32

D_FF = 5632
RMS_EPS = 1e-5
LN_EPS = 1e-5

LANES = 128
N_COL_BLOCKS = D_MODEL // LANES
VMEM_LIMIT = 56 * 1024 * 1024


def _params(n_axes):
    return pltpu.CompilerParams(dimension_semantics=("arbitrary",) * n_axes,
                                vmem_limit_bytes=VMEM_LIMIT)


def _resident(shape):
    return pl.BlockSpec(shape, lambda *_: (0,) * len(shape), pipeline_mode=pl.Buffered(1))


def _rms(x, g):
    return x * lax.rsqrt(jnp.mean(x * x, axis=-1, keepdims=True) + RMS_EPS) * g


def _sigmoid(x):
    return 1.0 / (1.0 + jnp.exp(-x))


def _ffn_kernel(x_ref, g_ref, wg_ref, wu_ref, wd_ref, gf_ref, o_ref, h_ref, acc_ref, *, final_norm):
    j = pl.program_id(1)

    @pl.when(j == 0)
    def _():
        x = x_ref[...]
        h_ref[...] = _rms(x, g_ref[...]).astype(BF16)
        acc_ref[...] = x

    h = h_ref[...]
    a_g = jnp.dot(h, wg_ref[...], preferred_element_type=F32)
    a_u = jnp.dot(h, wu_ref[...], preferred_element_type=F32)
    act = (a_g * _sigmoid(a_g)) * a_u
    acc_ref[...] += jnp.dot(act.astype(BF16), wd_ref[...], preferred_element_type=F32)

    @pl.when(j == pl.num_programs(1) - 1)
    def _():
        y = acc_ref[...]
        if final_norm:
            y = _rms(y, gf_ref[...])
        o_ref[...] = y


def _ffn(x, g, w_gate_up, w_down, g_final, *, final_norm, tm, tf=512):
    t = x.shape[0]
    tm = min(tm, t)
    nj = D_FF // tf
    return pl.pallas_call(
        functools.partial(_ffn_kernel, final_norm=final_norm),
        grid=(t // tm, nj),
        in_specs=[
            pl.BlockSpec((tm, D_MODEL), lambda i, j: (i, 0)),
            pl.BlockSpec((1, D_MODEL), lambda i, j: (0, 0)),
            pl.BlockSpec((D_MODEL, tf), lambda i, j: (0, j)),
            pl.BlockSpec((D_MODEL, tf), lambda i, j: (0, j + nj)),
            pl.BlockSpec((tf, D_MODEL), lambda i, j: (j, 0)),
            pl.BlockSpec((1, D_MODEL), lambda i, j: (0, 0)),
        ],
        out_specs=pl.BlockSpec((tm, D_MODEL), lambda i, j: (i, 0)),
        out_shape=jax.ShapeDtypeStruct((t, D_MODEL), F32),
        scratch_shapes=[pltpu.VMEM((tm, D_MODEL), BF16), pltpu.VMEM((tm, D_MODEL), F32)],
        compiler_params=_params(2),
        name="swiglu",
    )(x, g, w_gate_up, w_gate_up, w_down, g_final)


def _rope_tables(pos):
    inv_freq = ROPE_THETA ** (-jnp.arange(0, ROT_DIM, 2, dtype=F32) / ROT_DIM)
    ang = pos.astype(F32)[:, None] * inv_freq[None, :]
    cos, sin = jnp.cos(ang), jnp.sin(ang)
    half = ROT_DIM // 2
    rest = HEAD_DIM - ROT_DIM
    n = pos.shape[0]
    ones, zeros = jnp.ones((n, rest), F32), jnp.zeros((n, rest), F32)
    zh = jnp.zeros((n, half), F32)
    ca = jnp.concatenate([cos, cos, ones], axis=1)
    sb = jnp.concatenate([-sin, zh, zeros], axis=1)
    sc = jnp.concatenate([zh, sin, zeros], axis=1)
    rep = LANES // HEAD_DIM
    return tuple(jnp.tile(a, (1, rep)) for a in (ca, sb, sc))


def _qkv_kernel(x_ref, g_ref, w_ref, b_ref, ca_ref, sb_ref, sc_ref, q_ref, kv_ref):
    h = _rms(x_ref[...], g_ref[...]).astype(BF16)
    qkv = jnp.dot(h, w_ref[...], preferred_element_type=F32) + b_ref[...]
    ca, sb, sc = ca_ref[...], sb_ref[...], sc_ref[...]
    half = ROT_DIM // 2
    for s in range((Q_DIM + KV_DIM) // LANES):
        blk = qkv[:, s * LANES:(s + 1) * LANES]
        r = (blk * ca + pltpu.roll(blk, LANES - half, 1) * sb + pltpu.roll(blk, half, 1) * sc)
        if s < Q_DIM // LANES:
            q_ref[:, s * LANES:(s + 1) * LANES] = (r * (HEAD_DIM ** -0.5)).astype(BF16)
        else:
            c0 = s * LANES - Q_DIM
            kv_ref[:, c0:c0 + LANES] = r
    kv_ref[:, KV_DIM:] = qkv[:, Q_DIM + KV_DIM:]


def _qkv(x, g, w, b, tables, *, tm):
    t = x.shape[0]
    tm = min(tm, t)
    row = lambda i: (i, 0)
    const = lambda i: (0, 0)
    return pl.pallas_call(
        _qkv_kernel,
        grid=(t // tm,),
        in_specs=[
            pl.BlockSpec((tm, D_MODEL), row),
            pl.BlockSpec((1, D_MODEL), const),
            _resident((D_MODEL, QKV_DIM)),
            pl.BlockSpec((1, QKV_DIM), const),
            pl.BlockSpec((tm, LANES), row),
            pl.BlockSpec((tm, LANES), row),
            pl.BlockSpec((tm, LANES), row),
        ],
        out_specs=[pl.BlockSpec((tm, Q_DIM), row), pl.BlockSpec((tm, 2 * KV_DIM), row)],
        out_shape=[jax.ShapeDtypeStruct((t, Q_DIM), BF16),
                   jax.ShapeDtypeStruct((t, 2 * KV_DIM), F32)],
        compiler_params=_params(1),
        name="qkv_rope",
    )(x, g, w, b, *tables)


def _head_attention(qh, kg, vg, sink, valid):
    s = lax.dot_general(qh, kg, (((1,), (1,)), ((), ())), preferred_element_type=F32)
    if valid is not None:
        s = jnp.where(valid, s, -jnp.inf)
    m = jnp.maximum(jnp.max(s, axis=-1, keepdims=True), sink)
    p = jnp.exp(s - m)
    denom = jnp.sum(p, axis=-1, keepdims=True) + jnp.exp(sink - m)
    o = jnp.dot(p.astype(BF16), vg, preferred_element_type=F32)
    return o / denom


def _attn_prompt_kernel(sinks_ref, x_ref, q_ref, kvp_ref, kvc_ref, wo_ref, bo_ref, o_ref,
                        k_scr, v_scr, o_scr, *, tq):
    i = pl.program_id(0)
    k_scr[0:tq, :] = kvp_ref[:, :KV_DIM].astype(BF16)
    k_scr[tq:, :] = kvc_ref[:, :KV_DIM].astype(BF16)
    v_scr[0:tq, :] = kvp_ref[:, KV_DIM:].astype(BF16)
    v_scr[tq:, :] = kvc_ref[:, KV_DIM:].astype(BF16)
    n_prev = tq // CHUNK
    cq = lax.broadcasted_iota(jnp.int32, (tq, 2 * tq), 0) // CHUNK
    ck = lax.broadcasted_iota(jnp.int32, (tq, 2 * tq), 1) // CHUNK - n_prev
    valid = (ck <= cq) & (ck >= cq - WINDOW // CHUNK) & ((ck >= 0) | (i > 0))
    for h in range(N_HEADS):
        g = h // GROUP
        o_h = _head_attention(q_ref[:, h * HEAD_DIM:(h + 1) * HEAD_DIM],
                              k_scr[:, g * HEAD_DIM:(g + 1) * HEAD_DIM],
                              v_scr[:, g * HEAD_DIM:(g + 1) * HEAD_DIM],
                              sinks_ref[h], valid)
        o_scr[:, h * HEAD_DIM:(h + 1) * HEAD_DIM] = o_h.astype(BF16)
    y = jnp.dot(o_scr[...], wo_ref[...], preferred_element_type=F32) + bo_ref[...]
    o_ref[...] = x_ref[...] + y


def _attn_prompt(x, q, kv, sinks, wo, bo, *, tq=128):
    t = x.shape[0]
    assert tq >= WINDOW and t % tq == 0
    row = lambda i: (i, 0)
    const = lambda i: (0, 0)
    return pl.pallas_call(
        functools.partial(_attn_prompt_kernel, tq=tq),
        grid=(t // tq,),
        in_specs=[
            pl.BlockSpec(memory_space=pltpu.SMEM),
            pl.BlockSpec((tq, D_MODEL), row),
            pl.BlockSpec((tq, Q_DIM), row),
            pl.BlockSpec((tq, 2 * KV_DIM), lambda i: (jnp.maximum(i - 1, 0), 0)),
            pl.BlockSpec((tq, 2 * KV_DIM), row),
            _resident((Q_DIM, D_MODEL)),
            pl.BlockSpec((1, D_MODEL), const),
        ],
        out_specs=pl.BlockSpec((tq, D_MODEL), row),
        out_shape=jax.ShapeDtypeStruct((t, D_MODEL), F32),
        scratch_shapes=[pltpu.VMEM((2 * tq, KV_DIM), BF16), pltpu.VMEM((2 * tq, KV_DIM), BF16),
                        pltpu.VMEM((tq, Q_DIM), BF16)],
        compiler_params=_params(1),
        name="attn_prompt",
    )(sinks, x, q, kv, kv, wo, bo)


def _attn_sample_kernel(sinks_ref, x_ref, q_ref, kv_ref, ck_ref, cv_ref, wo_ref, bo_ref,
                        o_ref, nk_ref, nv_ref, k_scr, v_scr, o_scr, *, ts, wc):
    b = pl.program_id(0)
    k_new = kv_ref[:, :KV_DIM]
    v_new = kv_ref[:, KV_DIM:]
    k_scr[0:wc, :] = ck_ref[...].astype(BF16)
    k_scr[wc:, :] = k_new.astype(BF16)
    v_scr[0:wc, :] = cv_ref[...].astype(BF16)
    v_scr[wc:, :] = v_new.astype(BF16)
    nk_ref[0:wc - ts, :] = ck_ref[ts:, :]
    nk_ref[wc - ts:, :] = k_new
    nv_ref[0:wc - ts, :] = cv_ref[ts:, :]
    nv_ref[wc - ts:, :] = v_new
    row0 = pl.multiple_of(b * ts, ts)
    for h in range(N_HEADS):
        g = h // GROUP
        o_h = _head_attention(q_ref[:, h * HEAD_DIM:(h + 1) * HEAD_DIM],
                              k_scr[:, g * HEAD_DIM:(g + 1) * HEAD_DIM],
                              v_scr[:, g * HEAD_DIM:(g + 1) * HEAD_DIM],
                              sinks_ref[h], None)
        o_scr[pl.ds(row0, ts), h * HEAD_DIM:(h + 1) * HEAD_DIM] = o_h.astype(BF16)

    @pl.when(b == pl.num_programs(0) - 1)
    def _():
        y = jnp.dot(o_scr[...], wo_ref[...], preferred_element_type=F32) + bo_ref[...]
        o_ref[...] = x_ref[...] + y


def _attn_sample(x, q, kv, cache_k, cache_v, sinks, wo, bo, *, nb, ts):
    t = x.shape[0]
    wc = cache_k.shape[1]
    assert ts <= wc and ts % 16 == 0
    row = lambda b: (b, 0)
    const = lambda b: (0, 0)
    batch = lambda b: (b, 0, 0)
    return pl.pallas_call(
        functools.partial(_attn_sample_kernel, ts=ts, wc=wc),
        grid=(nb,),
        in_specs=[
            pl.BlockSpec(memory_space=pltpu.SMEM),
            pl.BlockSpec((t, D_MODEL), const),
            pl.BlockSpec((ts, Q_DIM), row),
            pl.BlockSpec((ts, 2 * KV_DIM), row),
            pl.BlockSpec((None, wc, KV_DIM), batch),
            pl.BlockSpec((None, wc, KV_DIM), batch),
            _resident((Q_DIM, D_MODEL)),
            pl.BlockSpec((1, D_MODEL), const),
        ],
        out_specs=[pl.BlockSpec((t, D_MODEL), const),
                   pl.BlockSpec((None, wc, KV_DIM), batch),
                   pl.BlockSpec((None, wc, KV_DIM), batch)],
        out_shape=[jax.ShapeDtypeStruct((t, D_MODEL), F32),
                   jax.ShapeDtypeStruct((nb, wc, KV_DIM), F32),
                   jax.ShapeDtypeStruct((nb, wc, KV_DIM), F32)],
        scratch_shapes=[pltpu.VMEM((wc + ts, KV_DIM), BF16), pltpu.VMEM((wc + ts, KV_DIM), BF16),
                        pltpu.VMEM((t, Q_DIM), BF16)],
        compiler_params=_params(1),
        name="attn_sample",
    )(sinks, x, q, kv, cache_k, cache_v, wo, bo)


def _pool_kernel(x_ref, pre_ref, g_ref, w_ref, sc_ref, o_ref, h_out_ref, ext_ref,
                 *, tm, pos0, rows_per_seq, raw_prefix):
    i = pl.program_id(0)
    x = x_ref[...]
    g = g_ref[...]
    h = _rms(x, g)
    if raw_prefix:
        pre = jnp.where(i > 0, _rms(pre_ref[...], g), 0.0)
    else:
        pre = pre_ref[...]
    ext_ref[0:POOL_PAD, :] = pre
    ext_ref[POOL_PAD:, :] = h
    h_out_ref[...] = h[tm - POOL_PAD:, :]
    row = lax.broadcasted_iota(jnp.int32, (tm, 1), 0)
    pos = pos0 + (i * tm + row) % rows_per_seq
    for gi, w in enumerate(POOL_WINDOWS):
        c0 = gi * POOL_GROUP_DIM
        sl = slice(c0, c0 + POOL_GROUP_DIM)
        acc = h[:, sl]
        for j in range(1, w):
            acc = acc + ext_ref[POOL_PAD - j:POOL_PAD - j + tm, sl]
        cnt = jnp.minimum(pos + 1, w).astype(F32)
        pooled = acc / cnt - h[:, sl]
        mixed = jnp.dot(pooled.astype(BF16), w_ref[gi], preferred_element_type=F32)
        o_ref[:, sl] = x[:, sl] + mixed * sc_ref[:, sl]


def _pool(x, prefix, g, w_group, scale, *, tm, pos0, rows_per_seq, raw_prefix):
    t = x.shape[0]
    tm = min(tm, t)
    nblk = tm // POOL_PAD
    if raw_prefix:
        pre_map = lambda i: (jnp.maximum(i * nblk - 1, 0), 0)
    else:
        pre_map = lambda i: (i, 0)
    row = lambda i: (i, 0)
    const = lambda i: (0, 0)
    return pl.pallas_call(
        functools.partial(_pool_kernel, tm=tm, pos0=pos0, rows_per_seq=rows_per_seq,
                          raw_prefix=raw_prefix),
        grid=(t // tm,),
        in_specs=[
            pl.BlockSpec((tm, D_MODEL), row),
            pl.BlockSpec((POOL_PAD, D_MODEL), pre_map),
            pl.BlockSpec((1, D_MODEL), const),
            pl.BlockSpec((len(POOL_WINDOWS), POOL_GROUP_DIM, POOL_GROUP_DIM), lambda i: (0, 0, 0)),
            pl.BlockSpec((1, D_MODEL), const),
        ],
        out_specs=[pl.BlockSpec((tm, D_MODEL), row), pl.BlockSpec((POOL_PAD, D_MODEL), row)],
        out_shape=[jax.ShapeDtypeStruct((t, D_MODEL), F32),
                   jax.ShapeDtypeStruct((t // tm * POOL_PAD, D_MODEL), F32)],
        scratch_shapes=[pltpu.VMEM((tm + POOL_PAD, D_MODEL), F32)],
        compiler_params=_params(1),
        name="pool_mix",
    )(x, prefix, g, w_group, scale)


def _glu(x_ref, g_ref, w1_ref, b1_ref):
    h = _rms(x_ref[...], g_ref[...]).astype(BF16)
    a = jnp.dot(h, w1_ref[...], preferred_element_type=F32) + b1_ref[...]
    return a[:, :D_MODEL] * _sigmoid(a[:, D_MODEL:])


def _dwconv_rows(ext_ref, wdw_ref, c_ref, *, n_rows, ext_row0, out_row0, rc):
    def col_block(cb, carry):
        for r0 in range(0, n_rows, rc):
            acc = jnp.zeros((rc, LANES), F32)
            for j in range(CONV_WIDTH):
                start = ext_row0 + r0 + j
                acc = acc + ext_ref[cb, start:start + rc, :] * wdw_ref[cb, j:j + 1, :]
            c_ref[cb, out_row0 + r0:out_row0 + r0 + rc, :] = acc
        return carry

    lax.fori_loop(0, N_COL_BLOCKS, col_block, 0)


def _conv_tail(x_ref, c_ref, bdw_ref, lng_ref, lnb_ref, w2_ref, b2_ref, o_ref):
    c = jnp.concatenate([c_ref[cb] for cb in range(N_COL_BLOCKS)], axis=1) + bdw_ref[...]
    mu = jnp.mean(c, axis=-1, keepdims=True)
    d = c - mu
    var = jnp.mean(d * d, axis=-1, keepdims=True)
    y = d * lax.rsqrt(var + LN_EPS) * lng_ref[...] + lnb_ref[...]
    z = y * _sigmoid(y)
    out = jnp.dot(z.astype(BF16), w2_ref[...], preferred_element_type=F32) + b2_ref[...]
    o_ref[...] = x_ref[...] + out


def _conv_prompt_kernel(x_ref, g_ref, w1_ref, b1_ref, wdw_ref, bdw_ref, lng_ref, lnb_ref,
                        w2_ref, b2_ref, o_ref, st_ref, ext_ref, c_ref, *, tm):
    i = pl.program_id(0)

    @pl.when(i == 0)
    def _():
        ext_ref[:, 0:CONV_PAD, :] = jnp.zeros((N_COL_BLOCKS, CONV_PAD, LANES), F32)

    u = _glu(x_ref, g_ref, w1_ref, b1_ref)
    for cb in range(N_COL_BLOCKS):
        ext_ref[cb, CONV_PAD:, :] = u[:, cb * LANES:(cb + 1) * LANES]
    _dwconv_rows(ext_ref, wdw_ref, c_ref, n_rows=tm, ext_row0=CONV_PAD - CONV_PREFIX,
                 out_row0=0, rc=64)
    _conv_tail(x_ref, c_ref, bdw_ref, lng_ref, lnb_ref, w2_ref, b2_ref, o_ref)
    for cb in range(N_COL_BLOCKS):
        tail = ext_ref[cb, tm:tm + CONV_PAD, :]
        ext_ref[cb, 0:CONV_PAD, :] = tail
        st_ref[:, cb * LANES:(cb + 1) * LANES] = tail


def _conv_weight_specs(const):
    return [
        pl.BlockSpec((1, D_MODEL), const),
        _resident((D_MODEL, 2 * D_MODEL)),
        pl.BlockSpec((1, 2 * D_MODEL), const),
        pl.BlockSpec((N_COL_BLOCKS, CONV_PAD, LANES), lambda i: (0, 0, 0)),
        pl.BlockSpec((1, D_MODEL), const),
        pl.BlockSpec((1, D_MODEL), const),
        pl.BlockSpec((1, D_MODEL), const),
        _resident((D_MODEL, D_MODEL)),
        pl.BlockSpec((1, D_MODEL), const),
    ]


def _conv_prompt(x, weights, *, tm):
    t = x.shape[0]
    tm = min(tm, t)
    assert tm >= CONV_PAD
    row = lambda i: (i, 0)
    const = lambda i: (0, 0)
    return pl.pallas_call(
        functools.partial(_conv_prompt_kernel, tm=tm),
        grid=(t // tm,),
        in_specs=[pl.BlockSpec((tm, D_MODEL), row)] + _conv_weight_specs(const),
        out_specs=[pl.BlockSpec((tm, D_MODEL), row), pl.BlockSpec((CONV_PAD, D_MODEL), const)],
        out_shape=[jax.ShapeDtypeStruct((t, D_MODEL), F32),
                   jax.ShapeDtypeStruct((CONV_PAD, D_MODEL), F32)],
        scratch_shapes=[pltpu.VMEM((N_COL_BLOCKS, tm + CONV_PAD, LANES), F32),
                        pltpu.VMEM((N_COL_BLOCKS, tm, LANES), F32)],
        compiler_params=_params(1),
        name="conv_prompt",
    )(x, *weights)


def _conv_sample_kernel(x_ref, st_in_ref, g_ref, w1_ref, b1_ref, wdw_ref, bdw_ref, lng_ref,
                        lnb_ref, w2_ref, b2_ref, o_ref, st_ref, ext_ref, c_ref, *, nb, ts):
    u = _glu(x_ref, g_ref, w1_ref, b1_ref)
    seg = CONV_PAD + ts
    for b in range(nb):
        for cb in range(N_COL_BLOCKS):
            cols = slice(cb * LANES, (cb + 1) * LANES)
            ext_ref[cb, b * seg:b * seg + CONV_PAD, :] = st_in_ref[b, :, cols]
            ext_ref[cb, b * seg + CONV_PAD:(b + 1) * seg, :] = u[b * ts:(b + 1) * ts, cols]
    for b in range(nb):
        _dwconv_rows(ext_ref, wdw_ref, c_ref, n_rows=ts,
                     ext_row0=b * seg + CONV_PAD - CONV_PREFIX, out_row0=b * ts, rc=ts)
    _conv_tail(x_ref, c_ref, bdw_ref, lng_ref, lnb_ref, w2_ref, b2_ref, o_ref)
    for b in range(nb):
        for cb in range(N_COL_BLOCKS):
            st_ref[b, :, cb * LANES:(cb + 1) * LANES] = ext_ref[cb, b * seg + ts:(b + 1) * seg, :]


def _conv_sample(x, state, weights, *, nb, ts):
    t = x.shape[0]
    assert ts % 8 == 0
    const = lambda i: (0, 0)
    const3 = lambda i: (0, 0, 0)
    return pl.pallas_call(
        functools.partial(_conv_sample_kernel, nb=nb, ts=ts),
        grid=(1,),
        in_specs=[pl.BlockSpec((t, D_MODEL), const),
                  pl.BlockSpec((nb, CONV_PAD, D_MODEL), const3)] + _conv_weight_specs(const),
        out_specs=[pl.BlockSpec((t, D_MODEL), const),
                   pl.BlockSpec((nb, CONV_PAD, D_MODEL), const3)],
        out_shape=[jax.ShapeDtypeStruct((t, D_MODEL), F32),
                   jax.ShapeDtypeStruct((nb, CONV_PAD, D_MODEL), F32)],
        scratch_shapes=[pltpu.VMEM((N_COL_BLOCKS, nb * (CONV_PAD + ts), LANES), F32),
                        pltpu.VMEM((N_COL_BLOCKS, t, LANES), F32)],
        compiler_params=_params(1),
        name="conv_sample",
    )(x, state, *weights)


def kernel(x_prompt, x_sample, cache_k, cache_v, state_pool, state_conv, norm_mix, norm_ffn, norm_final, a_w_qkv, a_b_qkv, a_sinks, a_w_o, a_b_o, b_w_group, b_scale, c_w_pw1, c_b_pw1, c_w_dw, c_b_dw, c_ln_g, c_ln_b, c_w_pw2, c_b_pw2, f_w_gate_up, f_w_down):
    bp, seq, _ = x_prompt.shape
    nb, ts, _ = x_sample.shape
    assert bp == 1 and seq % CHUNK == 0
    wc = cache_k.shape[2]
    xp = x_prompt.reshape(seq, D_MODEL)
    xs = x_sample.reshape(nb * ts, D_MODEL)
    row = lambda a: a.reshape(1, -1)

    tables_p = _rope_tables(jnp.arange(seq))
    tables_s = tuple(jnp.tile(a, (nb, 1)) for a in _rope_tables(PAST_LEN + jnp.arange(ts)))
    g_final = row(norm_final)

    kp_l, vp_l, ks_l, vs_l = [], [], [], []
    pp_l, ps_l, cp_l, cs_l = [], [], [], []
    for i in range(DEPTH):
        kind, j = i % N_MIXERS, i // N_MIXERS
        g_mix = row(norm_mix[i])
        if kind == 0:
            w_qkv, b_qkv = a_w_qkv[j].astype(BF16), row(a_b_qkv[j])
            w_o, b_o = a_w_o[j].astype(BF16), row(a_b_o[j])
            q, kv = _qkv(xp, g_mix, w_qkv, b_qkv, tables_p, tm=512)
            xp = _attn_prompt(xp, q, kv, a_sinks[j], w_o, b_o)
            keep = min(WINDOW, seq)
            kp_l.append(kv[seq - keep:, :KV_DIM].reshape(1, keep, N_KV_HEADS, HEAD_DIM))
            vp_l.append(kv[seq - keep:, KV_DIM:].reshape(1, keep, N_KV_HEADS, HEAD_DIM))
            q, kv = _qkv(xs, g_mix, w_qkv, b_qkv, tables_s, tm=nb * ts)
            xs, nk, nv = _attn_sample(xs, q, kv, cache_k[j].reshape(nb, wc, KV_DIM),
                                      cache_v[j].reshape(nb, wc, KV_DIM), a_sinks[j], w_o, b_o,
                                      nb=nb, ts=ts)
            ks_l.append(nk.reshape(nb, wc, N_KV_HEADS, HEAD_DIM))
            vs_l.append(nv.reshape(nb, wc, N_KV_HEADS, HEAD_DIM))
        elif kind == 1:
            w_g, sc = b_w_group[j].astype(BF16), row(b_scale[j])
            xp, hp = _pool(xp, xp, g_mix, w_g, sc, tm=256, pos0=0, rows_per_seq=seq,
                           raw_prefix=True)
            pp_l.append(hp[-POOL_PREFIX:].reshape(1, POOL_PREFIX, D_MODEL))
            assert ts == POOL_PAD
            pre = jnp.pad(state_pool[j], ((0, 0), (POOL_PAD - POOL_PREFIX, 0), (0, 0)))
            xs, hs = _pool(xs, pre.reshape(nb * POOL_PAD, D_MODEL), g_mix, w_g, sc, tm=ts,
                           pos0=PAST_LEN, rows_per_seq=ts, raw_prefix=False)
            ext = jnp.concatenate([state_pool[j], hs.reshape(nb, ts, D_MODEL)], axis=1)
            ps_l.append(ext[:, -POOL_PREFIX:])
        else:
            w_dw = jnp.pad(c_w_dw[j], ((0, CONV_PAD - CONV_WIDTH), (0, 0)))
            w_dw = w_dw.reshape(CONV_PAD, N_COL_BLOCKS, LANES).transpose(1, 0, 2)
            weights = (g_mix, c_w_pw1[j].astype(BF16), row(c_b_pw1[j]), w_dw, row(c_b_dw[j]),
                       row(c_ln_g[j]), row(c_ln_b[j]), c_w_pw2[j].astype(BF16), row(c_b_pw2[j]))
            xp, st = _conv_prompt(xp, weights, tm=256)
            cp_l.append(st[-CONV_PREFIX:].reshape(1, CONV_PREFIX, D_MODEL))
            pre = jnp.pad(state_conv[j], ((0, 0), (CONV_PAD - CONV_PREFIX, 0), (0, 0)))
            xs, st = _conv_sample(xs, pre, weights, nb=nb, ts=ts)
            cs_l.append(st[:, -CONV_PREFIX:])
        last = i == DEPTH - 1
        g_ffn = row(norm_ffn[i])
        w_gu, w_d = f_w_gate_up[i].astype(BF16), f_w_down[i].astype(BF16)
        xp = _ffn(xp, g_ffn, w_gu, w_d, g_final, final_norm=last, tm=512)
        xs = _ffn(xs, g_ffn, w_gu, w_d, g_final, final_norm=last, tm=256)
    return (xp.reshape(1, seq, D_MODEL), xs.reshape(nb, ts, D_MODEL),
            jnp.stack(kp_l), jnp.stack(vp_l), jnp.stack(ks_l), jnp.stack(vs_l),
            jnp.stack(pp_l), jnp.stack(ps_l), jnp.stack(cp_l), jnp.stack(cs_l))
```

```python
import functools

import jax
import jax.numpy as jnp
from jax import lax
from jax.experimental import pallas as pl
from jax.experimental.pallas import tpu as pltpu

F32 = jnp.float32
BF16 = jnp.bfloat16

D_MODEL = 2048
DEPTH = 4
PAST_LEN = 1024
CHUNK = 64
N_MIXERS = 3

N_HEADS = 32
N_KV_HEADS = 4
HEAD_DIM = 64
GROUP = N_HEADS // N_KV_HEADS
Q_DIM = N_HEADS * HEAD_DIM
KV_DIM = N_KV_HEADS * HEAD_DIM
QKV_DIM = Q_DIM + 2 * KV_DIM
WINDOW = 128
ROT_DIM = HEAD_DIM // 4
ROPE_THETA = 500000.0

POOL_WINDOWS = (2, 4, 8, 16)
POOL_GROUP_DIM = D_MODEL // len(POOL_WINDOWS)
POOL_PREFIX = max(POOL_WINDOWS) - 1
POOL_PAD = 16

CONV_WIDTH = 31
CONV_PREFIX = CONV_WIDTH - 1
CONV_PAD = 32

D_FF = 5632
RMS_EPS = 1e-5
LN_EPS = 1e-5

LANES = 128
N_COL_BLOCKS = D_MODEL // LANES
VMEM_LIMIT = 56 * 1024 * 1024


def _params(n_axes):
    return pltpu.CompilerParams(dimension_semantics=("arbitrary",) * n_axes,
                                vmem_limit_bytes=VMEM_LIMIT)


def _resident(shape, layer):
    return pl.BlockSpec((None,) + shape, lambda *_: (layer,) + (0,) * len(shape),
                        pipeline_mode=pl.Buffered(1))


def _rms(x, g):
    return x * lax.rsqrt(jnp.mean(x * x, axis=-1, keepdims=True) + RMS_EPS) * g


def _sigmoid(x):
    return 1.0 / (1.0 + jnp.exp(-x))


def _ffn_kernel(x_ref, g_ref, wg_ref, wu_ref, wd_ref, gf_ref, o_ref, h_ref, *, final_norm):
    j = pl.program_id(1)

    @pl.when(j == 0)
    def _():
        x = x_ref[...]
        h_ref[...] = _rms(x, g_ref[...]).astype(BF16)
        o_ref[...] = x

    h = h_ref[...]
    a_g = jnp.dot(h, wg_ref[...], preferred_element_type=F32)
    a_u = jnp.dot(h, wu_ref[...], preferred_element_type=F32)
    act = (a_g * _sigmoid(a_g)) * a_u
    o_ref[...] += jnp.dot(act.astype(BF16), wd_ref[...], preferred_element_type=F32)

    if final_norm:
        @pl.when(j == pl.num_programs(1) - 1)
        def _():
            o_ref[...] = _rms(o_ref[...], gf_ref[...])


def _ffn(x, g, w_gate_up, w_down, g_final, *, layer, final_norm, tm, tf=512):
    t = x.shape[0]
    tm = min(tm, t)
    nj = D_FF // tf
    return pl.pallas_call(
        functools.partial(_ffn_kernel, final_norm=final_norm),
        grid=(t // tm, nj),
        in_specs=[
            pl.BlockSpec((tm, D_MODEL), lambda i, j: (i, 0)),
            pl.BlockSpec((1, D_MODEL), lambda i, j: (0, 0)),
            pl.BlockSpec((None, D_MODEL, tf), lambda i, j: (layer, 0, j)),
            pl.BlockSpec((None, D_MODEL, tf), lambda i, j: (layer, 0, j + nj)),
            pl.BlockSpec((None, tf, D_MODEL), lambda i, j: (layer, j, 0)),
            pl.BlockSpec((1, D_MODEL), lambda i, j: (0, 0)),
        ],
        out_specs=pl.BlockSpec((tm, D_MODEL), lambda i, j: (i, 0)),
        out_shape=jax.ShapeDtypeStruct((t, D_MODEL), F32),
        scratch_shapes=[pltpu.VMEM((tm, D_MODEL), BF16)],
        compiler_params=_params(2),
        name="swiglu",
    )(x, g, w_gate_up, w_gate_up, w_down, g_final)


def _rope_tables(pos):
    inv_freq = ROPE_THETA ** (-jnp.arange(0, ROT_DIM, 2, dtype=F32) / ROT_DIM)
    ang = pos.astype(F32)[:, None] * inv_freq[None, :]
    cos, sin = jnp.cos(ang), jnp.sin(ang)
    half = ROT_DIM // 2
    rest = HEAD_DIM - ROT_DIM
    n = pos.shape[0]
    ones, zeros = jnp.ones((n, rest), F32), jnp.zeros((n, rest), F32)
    zh = jnp.zeros((n, half), F32)
    ca = jnp.concatenate([cos, cos, ones], axis=1)
    sb = jnp.concatenate([-sin, zh, zeros], axis=1)
    sc = jnp.concatenate([zh, sin, zeros], axis=1)
    rep = LANES // HEAD_DIM
    return tuple(jnp.tile(a, (1, rep)) for a in (ca, sb, sc))


def _qkv_kernel(x_ref, g_ref, w_ref, b_ref, ca_ref, sb_ref, sc_ref, q_ref, kv_ref):
    h = _rms(x_ref[...], g_ref[...]).astype(BF16)
    qkv = jnp.dot(h, w_ref[...], preferred_element_type=F32) + b_ref[...]
    ca, sb, sc = ca_ref[...], sb_ref[...], sc_ref[...]
    half = ROT_DIM // 2
    for s in range((Q_DIM + KV_DIM) // LANES):
        blk = qkv[:, s * LANES:(s + 1) * LANES]
        r = (blk * ca + pltpu.roll(blk, LANES - half, 1) * sb + pltpu.roll(blk, half, 1) * sc)
        if s < Q_DIM // LANES:
            q_ref[:, s * LANES:(s + 1) * LANES] = (r * (HEAD_DIM ** -0.5)).astype(BF16)
        else:
            c0 = s * LANES - Q_DIM
            kv_ref[:, c0:c0 + LANES] = r
    kv_ref[:, KV_DIM:] = qkv[:, Q_DIM + KV_DIM:]


def _qkv(x, g, w, b, tables, *, layer, tm):
    t = x.shape[0]
    tm = min(tm, t)
    row = lambda i: (i, 0)
    const = lambda i: (0, 0)
    return pl.pallas_call(
        _qkv_kernel,
        grid=(t // tm,),
        in_specs=[
            pl.BlockSpec((tm, D_MODEL), row),
            pl.BlockSpec((1, D_MODEL), const),
            _resident((D_MODEL, QKV_DIM), layer),
            pl.BlockSpec((1, QKV_DIM), const),
            pl.BlockSpec((tm, LANES), row),
            pl.BlockSpec((tm, LANES), row),
            pl.BlockSpec((tm, LANES), row),
        ],
        out_specs=[pl.BlockSpec((tm, Q_DIM), row), pl.BlockSpec((tm, 2 * KV_DIM), row)],
        out_shape=[jax.ShapeDtypeStruct((t, Q_DIM), BF16),
                   jax.ShapeDtypeStruct((t, 2 * KV_DIM), F32)],
        compiler_params=_params(1),
        name="qkv_rope",
    )(x, g, w, b, *tables)


def _half_masked(blk):
    lo = lax.broadcasted_iota(jnp.int32, blk.shape, 1) < HEAD_DIM
    swapped = pltpu.roll(blk, HEAD_DIM, 1)
    zero = jnp.zeros_like(blk)
    sides = ((jnp.where(lo, blk, zero), jnp.where(lo, zero, swapped)),
             (jnp.where(lo, swapped, zero), jnp.where(lo, zero, blk)))
    return [tuple(x.astype(BF16) for x in side) for side in sides]


def _softmax_block(s, sink, valid):
    if valid is not None:
        s = jnp.where(valid, s, -jnp.inf)
    m = jnp.maximum(jnp.max(s, axis=-1, keepdims=True), sink)
    p = jnp.exp(s - m)
    denom = jnp.sum(p, axis=-1, keepdims=True) + jnp.exp(sink - m)
    return p.astype(BF16), 1.0 / denom


def _attention(q_ref, k_parts, v_parts, sinks_ref, valid, store):
    r = q_ref.shape[0]
    pairs = GROUP // 2
    nt = (((1,), (1,)), ((), ()))

    def gather(parts, m):
        return jnp.concatenate([ref[:, off + m * LANES:off + (m + 1) * LANES]
                                for ref, off in parts], axis=0)

    for m in range(KV_DIM // LANES):
        k_ops = _half_masked(gather(k_parts, m))
        v_ops = _half_masked(gather(v_parts, m))
        for gi in range(LANES // HEAD_DIM):
            p0 = (m * (LANES // HEAD_DIM) + gi) * pairs
            qg = jnp.concatenate([q_ref[:, (p0 + pi) * LANES:(p0 + pi + 1) * LANES]
                                  for pi in range(pairs)], axis=0)
            pv, rcp = [], []
            for par in range(2):
                s = lax.dot_general(qg, k_ops[gi][par], nt, preferred_element_type=F32)
                blocks = [_softmax_block(s[pi * r:(pi + 1) * r], sinks_ref[2 * (p0 + pi) + par],
                                         valid) for pi in range(pairs)]
                p_all = jnp.concatenate([b[0] for b in blocks], axis=0)
                pv.append(jnp.dot(p_all, v_ops[gi][par], preferred_element_type=F32))
                rcp.append([b[1] for b in blocks])
            for pi in range(pairs):
                rows = slice(pi * r, (pi + 1) * r)
                o = pv[0][rows] * rcp[0][pi] + pv[1][rows] * rcp[1][pi]
                store(p0 + pi, o.astype(BF16))


def _attn_prompt_kernel(sinks_ref, x_ref, q_ref, kvp_ref, kvc_ref, wo_ref, bo_ref, o_ref,
                        o_scr, *, tq):
    i = pl.program_id(0)
    n_prev = tq // CHUNK
    cq = lax.broadcasted_iota(jnp.int32, (tq, 2 * tq), 0) // CHUNK
    ck = lax.broadcasted_iota(jnp.int32, (tq, 2 * tq), 1) // CHUNK - n_prev
    valid = (ck <= cq) & (ck >= cq - WINDOW // CHUNK) & ((ck >= 0) | (i > 0))

    def store(pair, val):
        o_scr[:, pair * LANES:(pair + 1) * LANES] = val

    _attention(q_ref, [(kvp_ref, 0), (kvc_ref, 0)], [(kvp_ref, KV_DIM), (kvc_ref, KV_DIM)],
               sinks_ref, valid, store)
    y = jnp.dot(o_scr[...], wo_ref[...], preferred_element_type=F32) + bo_ref[...]
    o_ref[...] = x_ref[...] + y


def _attn_prompt(x, q, kv, sinks, wo, bo, *, layer, tq=128):
    t = x.shape[0]
    assert tq >= WINDOW and t % tq == 0
    row = lambda i: (i, 0)
    const = lambda i: (0, 0)
    return pl.pallas_call(
        functools.partial(_attn_prompt_kernel, tq=tq),
        grid=(t // tq,),
        in_specs=[
            pl.BlockSpec(memory_space=pltpu.SMEM),
            pl.BlockSpec((tq, D_MODEL), row),
            pl.BlockSpec((tq, Q_DIM), row),
            pl.BlockSpec((tq, 2 * KV_DIM), lambda i: (jnp.maximum(i - 1, 0), 0)),
            pl.BlockSpec((tq, 2 * KV_DIM), row),
            _resident((Q_DIM, D_MODEL), layer),
            pl.BlockSpec((1, D_MODEL), const),
        ],
        out_specs=pl.BlockSpec((tq, D_MODEL), row),
        out_shape=jax.ShapeDtypeStruct((t, D_MODEL), F32),
        scratch_shapes=[pltpu.VMEM((tq, Q_DIM), BF16)],
        compiler_params=_params(1),
        name="attn_prompt",
    )(sinks, x, q, kv, kv, wo, bo)


def _attn_sample_kernel(sinks_ref, x_ref, q_ref, kv_ref, ck_ref, cv_ref, wo_ref, bo_ref,
                        o_ref, nk_ref, nv_ref, o_scr, *, ts, wc):
    b = pl.program_id(0)
    nk_ref[0:wc - ts, :] = ck_ref[ts:, :]
    nk_ref[wc - ts:, :] = kv_ref[:, :KV_DIM]
    nv_ref[0:wc - ts, :] = cv_ref[ts:, :]
    nv_ref[wc - ts:, :] = kv_ref[:, KV_DIM:]
    row0 = pl.multiple_of(b * ts, ts)

    def store(pair, val):
        o_scr[pl.ds(row0, ts), pair * LANES:(pair + 1) * LANES] = val

    _attention(q_ref, [(ck_ref, 0), (kv_ref, 0)], [(cv_ref, 0), (kv_ref, KV_DIM)],
               sinks_ref, None, store)

    @pl.when(b == pl.num_programs(0) - 1)
    def _():
        y = jnp.dot(o_scr[...], wo_ref[...], preferred_element_type=F32) + bo_ref[...]
        o_ref[...] = x_ref[...] + y


def _attn_sample(x, q, kv, cache_k, cache_v, sinks, wo, bo, *, layer, nb, ts):
    t = x.shape[0]
    wc = cache_k.shape[1]
    assert ts <= wc and ts % 16 == 0
    row = lambda b: (b, 0)
    const = lambda b: (0, 0)
    batch = lambda b: (b, 0, 0)
    return pl.pallas_call(
        functools.partial(_attn_sample_kernel, ts=ts, wc=wc),
        grid=(nb,),
        in_specs=[
            pl.BlockSpec(memory_space=pltpu.SMEM),
            pl.BlockSpec((t, D_MODEL), const),
            pl.BlockSpec((ts, Q_DIM), row),
            pl.BlockSpec((ts, 2 * KV_DIM), row),
            pl.BlockSpec((None, wc, KV_DIM), batch),
            pl.BlockSpec((None, wc, KV_DIM), batch),
            _resident((Q_DIM, D_MODEL), layer),
            pl.BlockSpec((1, D_MODEL), const),
        ],
        out_specs=[pl.BlockSpec((t, D_MODEL), const),
                   pl.BlockSpec((None, wc, KV_DIM), batch),
                   pl.BlockSpec((None, wc, KV_DIM), batch)],
        out_shape=[jax.ShapeDtypeStruct((t, D_MODEL), F32),
                   jax.ShapeDtypeStruct((nb, wc, KV_DIM), F32),
                   jax.ShapeDtypeStruct((nb, wc, KV_DIM), F32)],
        scratch_shapes=[pltpu.VMEM((t, Q_DIM), BF16)],
        compiler_params=_params(1),
        name="attn_sample",
    )(sinks, x, q, kv, cache_k, cache_v, wo, bo)


def _pool_kernel(x_ref, pre_ref, g_ref, w_ref, sc_ref, o_ref, h_out_ref, ext_ref,
                 *, tm, pos0, rows_per_seq, raw_prefix):
    i = pl.program_id(0)
    x = x_ref[...]
    g = g_ref[...]
    h = _rms(x, g)
    if raw_prefix:
        pre = jnp.where(i > 0, _rms(pre_ref[...], g), 0.0)
    else:
        pre = pre_ref[...]
    ext_ref[0:POOL_PAD, :] = pre
    ext_ref[POOL_PAD:, :] = h
    h_out_ref[...] = h[tm - POOL_PAD:, :]
    row = lax.broadcasted_iota(jnp.int32, (tm, 1), 0)
    pos = pos0 + (i * tm + row) % rows_per_seq
    for gi, w in enumerate(POOL_WINDOWS):
        c0 = gi * POOL_GROUP_DIM
        sl = slice(c0, c0 + POOL_GROUP_DIM)
        acc = h[:, sl]
        for j in range(1, w):
            acc = acc + ext_ref[POOL_PAD - j:POOL_PAD - j + tm, sl]
        cnt = jnp.minimum(pos + 1, w).astype(F32)
        pooled = acc / cnt - h[:, sl]
        mixed = jnp.dot(pooled.astype(BF16), w_ref[gi], preferred_element_type=F32)
        o_ref[:, sl] = x[:, sl] + mixed * sc_ref[:, sl]


def _pool(x, prefix, g, w_group, scale, *, layer, tm, pos0, rows_per_seq, raw_prefix):
    t = x.shape[0]
    tm = min(tm, t)
    nblk = tm // POOL_PAD
    if raw_prefix:
        pre_map = lambda i: (jnp.maximum(i * nblk - 1, 0), 0)
    else:
        pre_map = lambda i: (i, 0)
    row = lambda i: (i, 0)
    const = lambda i: (0, 0)
    return pl.pallas_call(
        functools.partial(_pool_kernel, tm=tm, pos0=pos0, rows_per_seq=rows_per_seq,
                          raw_prefix=raw_prefix),
        grid=(t // tm,),
        in_specs=[
            pl.BlockSpec((tm, D_MODEL), row),
            pl.BlockSpec((POOL_PAD, D_MODEL), pre_map),
            pl.BlockSpec((1, D_MODEL), const),
            _resident((len(POOL_WINDOWS), POOL_GROUP_DIM, POOL_GROUP_DIM), layer),
            pl.BlockSpec((1, D_MODEL), const),
        ],
        out_specs=[pl.BlockSpec((tm, D_MODEL), row), pl.BlockSpec((POOL_PAD, D_MODEL), row)],
        out_shape=[jax.ShapeDtypeStruct((t, D_MODEL), F32),
                   jax.ShapeDtypeStruct((t // tm * POOL_PAD, D_MODEL), F32)],
        scratch_shapes=[pltpu.VMEM((tm + POOL_PAD, D_MODEL), F32)],
        compiler_params=_params(1),
        name="pool_mix",
    )(x, prefix, g, w_group, scale)


def _glu(x_ref, g_ref, w1_ref, b1_ref):
    h = _rms(x_ref[...], g_ref[...]).astype(BF16)
    a = jnp.dot(h, w1_ref[...], preferred_element_type=F32) + b1_ref[...]
    return a[:, :D_MODEL] * _sigmoid(a[:, D_MODEL:])


def _dwconv_rows(ext_ref, wdw_ref, c_ref, *, n_rows, ext_row0, out_row0, rc):
    def col_block(cb, carry):
        for r0 in range(0, n_rows, rc):
            acc = jnp.zeros((rc, LANES), F32)
            for j in range(CONV_WIDTH):
                start = ext_row0 + r0 + j
                acc = acc + ext_ref[cb, start:start + rc, :] * wdw_ref[cb, j:j + 1, :]
            c_ref[cb, out_row0 + r0:out_row0 + r0 + rc, :] = acc
        return carry

    lax.fori_loop(0, N_COL_BLOCKS, col_block, 0)


def _conv_tail(x_ref, c_ref, bdw_ref, lng_ref, lnb_ref, w2_ref, b2_ref, o_ref):
    c = jnp.concatenate([c_ref[cb] for cb in range(N_COL_BLOCKS)], axis=1) + bdw_ref[...]
    mu = jnp.mean(c, axis=-1, keepdims=True)
    d = c - mu
    var = jnp.mean(d * d, axis=-1, keepdims=True)
    y = d * lax.rsqrt(var + LN_EPS) * lng_ref[...] + lnb_ref[...]
    z = y * _sigmoid(y)
    out = jnp.dot(z.astype(BF16), w2_ref[...], preferred_element_type=F32) + b2_ref[...]
    o_ref[...] = x_ref[...] + out


def _conv_prompt_kernel(x_ref, g_ref, w1_ref, b1_ref, wdw_ref, bdw_ref, lng_ref, lnb_ref,
                        w2_ref, b2_ref, o_ref, st_ref, ext_ref, c_ref, *, tm):
    i = pl.program_id(0)

    @pl.when(i == 0)
    def _():
        ext_ref[:, 0:CONV_PAD, :] = jnp.zeros((N_COL_BLOCKS, CONV_PAD, LANES), F32)

    u = _glu(x_ref, g_ref, w1_ref, b1_ref)
    for cb in range(N_COL_BLOCKS):
        ext_ref[cb, CONV_PAD:, :] = u[:, cb * LANES:(cb + 1) * LANES]
    _dwconv_rows(ext_ref, wdw_ref, c_ref, n_rows=tm, ext_row0=CONV_PAD - CONV_PREFIX,
                 out_row0=0, rc=64)
    _conv_tail(x_ref, c_ref, bdw_ref, lng_ref, lnb_ref, w2_ref, b2_ref, o_ref)
    for cb in range(N_COL_BLOCKS):
        tail = ext_ref[cb, tm:tm + CONV_PAD, :]
        ext_ref[cb, 0:CONV_PAD, :] = tail
        st_ref[:, cb * LANES:(cb + 1) * LANES] = tail


def _conv_weight_specs(const, layer):
    return [
        pl.BlockSpec((1, D_MODEL), const),
        _resident((D_MODEL, 2 * D_MODEL), layer),
        pl.BlockSpec((1, 2 * D_MODEL), const),
        pl.BlockSpec((N_COL_BLOCKS, CONV_PAD, LANES), lambda i: (0, 0, 0)),
        pl.BlockSpec((1, D_MODEL), const),
        pl.BlockSpec((1, D_MODEL), const),
        pl.BlockSpec((1, D_MODEL), const),
        _resident((D_MODEL, D_MODEL), layer),
        pl.BlockSpec((1, D_MODEL), const),
    ]


def _conv_prompt(x, weights, *, layer, tm):
    t = x.shape[0]
    tm = min(tm, t)
    assert tm >= CONV_PAD
    row = lambda i: (i, 0)
    const = lambda i: (0, 0)
    return pl.pallas_call(
        functools.partial(_conv_prompt_kernel, tm=tm),
        grid=(t // tm,),
        in_specs=[pl.BlockSpec((tm, D_MODEL), row)] + _conv_weight_specs(const, layer),
        out_specs=[pl.BlockSpec((tm, D_MODEL), row), pl.BlockSpec((CONV_PAD, D_MODEL), const)],
        out_shape=[jax.ShapeDtypeStruct((t, D_MODEL), F32),
                   jax.ShapeDtypeStruct((CONV_PAD, D_MODEL), F32)],
        scratch_shapes=[pltpu.VMEM((N_COL_BLOCKS, tm + CONV_PAD, LANES), F32),
                        pltpu.VMEM((N_COL_BLOCKS, tm, LANES), F32)],
        compiler_params=_params(1),
        name="conv_prompt",
    )(x, *weights)


def _conv_sample_kernel(x_ref, st_in_ref, g_ref, w1_ref, b1_ref, wdw_ref, bdw_ref, lng_ref,
                        lnb_ref, w2_ref, b2_ref, o_ref, st_ref, ext_ref, c_ref, *, nb, ts):
    u = _glu(x_ref, g_ref, w1_ref, b1_ref)
    seg = CONV_PAD + ts
    for b in range(nb):
        for cb in range(N_COL_BLOCKS):
            cols = slice(cb * LANES, (cb + 1) * LANES)
            ext_ref[cb, b * seg:b * seg + CONV_PAD, :] = st_in_ref[b, :, cols]
            ext_ref[cb, b * seg + CONV_PAD:(b + 1) * seg, :] = u[b * ts:(b + 1) * ts, cols]
    for b in range(nb):
        _dwconv_rows(ext_ref, wdw_ref, c_ref, n_rows=ts,
                     ext_row0=b * seg + CONV_PAD - CONV_PREFIX, out_row0=b * ts, rc=ts)
    _conv_tail(x_ref, c_ref, bdw_ref, lng_ref, lnb_ref, w2_ref, b2_ref, o_ref)
    for b in range(nb):
        for cb in range(N_COL_BLOCKS):
            st_ref[b, :, cb * LANES:(cb + 1) * LANES] = ext_ref[cb, b * seg + ts:(b + 1) * seg, :]


def _conv_sample(x, state, weights, *, layer, nb, ts):
    t = x.shape[0]
    assert ts % 8 == 0
    const = lambda i: (0, 0)
    const3 = lambda i: (0, 0, 0)
    return pl.pallas_call(
        functools.partial(_conv_sample_kernel, nb=nb, ts=ts),
        grid=(1,),
        in_specs=[pl.BlockSpec((t, D_MODEL), const),
                  pl.BlockSpec((nb, CONV_PAD, D_MODEL), const3)] + _conv_weight_specs(const, layer),
        out_specs=[pl.BlockSpec((t, D_MODEL), const),
                   pl.BlockSpec((nb, CONV_PAD, D_MODEL), const3)],
        out_shape=[jax.ShapeDtypeStruct((t, D_MODEL), F32),
                   jax.ShapeDtypeStruct((nb, CONV_PAD, D_MODEL), F32)],
        scratch_shapes=[pltpu.VMEM((N_COL_BLOCKS, nb * (CONV_PAD + ts), LANES), F32),
                        pltpu.VMEM((N_COL_BLOCKS, t, LANES), F32)],
        compiler_params=_params(1),
        name="conv_sample",
    )(x, state, *weights)


def kernel(x_prompt, x_sample, cache_k, cache_v, state_pool, state_conv, norm_mix, norm_ffn, norm_final, a_w_qkv, a_b_qkv, a_sinks, a_w_o, a_b_o, b_w_group, b_scale, c_w_pw1, c_b_pw1, c_w_dw, c_b_dw, c_ln_g, c_ln_b, c_w_pw2, c_b_pw2, f_w_gate_up, f_w_down):
    bp, seq, _ = x_prompt.shape
    nb, ts, _ = x_sample.shape
    assert bp == 1 and seq % CHUNK == 0
    wc = cache_k.shape[2]
    xp = x_prompt.reshape(seq, D_MODEL)
    xs = x_sample.reshape(nb * ts, D_MODEL)
    row = lambda a: a.reshape(1, -1)

    tables_p = _rope_tables(jnp.arange(seq))
    tables_s = tuple(jnp.tile(a, (nb, 1)) for a in _rope_tables(PAST_LEN + jnp.arange(ts)))
    g_final = row(norm_final)

    w_qkv, w_o = a_w_qkv.astype(BF16), a_w_o.astype(BF16)
    w_g = b_w_group.astype(BF16)
    w_pw1, w_pw2 = c_w_pw1.astype(BF16), c_w_pw2.astype(BF16)
    w_gu, w_d = f_w_gate_up.astype(BF16), f_w_down.astype(BF16)

    kp_l, vp_l, ks_l, vs_l = [], [], [], []
    pp_l, ps_l, cp_l, cs_l = [], [], [], []
    for i in range(DEPTH):
        kind, j = i % N_MIXERS, i // N_MIXERS
        g_mix = row(norm_mix[i])
        if kind == 0:
            b_qkv, b_o = row(a_b_qkv[j]), row(a_b_o[j])
            q, kv = _qkv(xp, g_mix, w_qkv, b_qkv, tables_p, layer=j, tm=512)
            xp = _attn_prompt(xp, q, kv, a_sinks[j], w_o, b_o, layer=j)
            keep = min(WINDOW, seq)
            kp_l.append(kv[seq - keep:, :KV_DIM].reshape(1, keep, N_KV_HEADS, HEAD_DIM))
            vp_l.append(kv[seq - keep:, KV_DIM:].reshape(1, keep, N_KV_HEADS, HEAD_DIM))
            q, kv = _qkv(xs, g_mix, w_qkv, b_qkv, tables_s, layer=j, tm=nb * ts)
            xs, nk, nv = _attn_sample(xs, q, kv, cache_k[j].reshape(nb, wc, KV_DIM),
                                      cache_v[j].reshape(nb, wc, KV_DIM), a_sinks[j], w_o, b_o,
                                      layer=j, nb=nb, ts=ts)
            ks_l.append(nk.reshape(nb, wc, N_KV_HEADS, HEAD_DIM))
            vs_l.append(nv.reshape(nb, wc, N_KV_HEADS, HEAD_DIM))
        elif kind == 1:
            sc = row(b_scale[j])
            xp, hp = _pool(xp, xp, g_mix, w_g, sc, layer=j, tm=256, pos0=0, rows_per_seq=seq,
                           raw_prefix=True)
            pp_l.append(hp[-POOL_PREFIX:].reshape(1, POOL_PREFIX, D_MODEL))
            assert ts == POOL_PAD
            pre = jnp.pad(state_pool[j], ((0, 0), (POOL_PAD - POOL_PREFIX, 0), (0, 0)))
            xs, hs = _pool(xs, pre.reshape(nb * POOL_PAD, D_MODEL), g_mix, w_g, sc, layer=j,
                           tm=ts, pos0=PAST_LEN, rows_per_seq=ts, raw_prefix=False)
            ext = jnp.concatenate([state_pool[j], hs.reshape(nb, ts, D_MODEL)], axis=1)
            ps_l.append(ext[:, -POOL_PREFIX:])
        else:
            w_dw = jnp.pad(c_w_dw[j], ((0, CONV_PAD - CONV_WIDTH), (0, 0)))
            w_dw = w_dw.reshape(CONV_PAD, N_COL_BLOCKS, LANES).transpose(1, 0, 2)
            weights = (g_mix, w_pw1, row(c_b_pw1[j]), w_dw, row(c_b_dw[j]),
                       row(c_ln_g[j]), row(c_ln_b[j]), w_pw2, row(c_b_pw2[j]))
            xp, st = _conv_prompt(xp, weights, layer=j, tm=256)
            cp_l.append(st[-CONV_PREFIX:].reshape(1, CONV_PREFIX, D_MODEL))
            pre = jnp.pad(state_conv[j], ((0, 0), (CONV_PAD - CONV_PREFIX, 0), (0, 0)))
            xs, st = _conv_sample(xs, pre, weights, layer=j, nb=nb, ts=ts)
            cs_l.append(st[:, -CONV_PREFIX:])
        last = i == DEPTH - 1
        g_ffn = row(norm_ffn[i])
        xp = _ffn(xp, g_ffn, w_gu, w_d, g_final, layer=i, final_norm=last, tm=512)
        xs = _ffn(xs, g_ffn, w_gu, w_d, g_final, layer=i, final_norm=last, tm=256)
    return (xp.reshape(1, seq, D_MODEL), xs.reshape(nb, ts, D_MODEL),
            jnp.stack(kp_l), jnp.stack(vp_l), jnp.stack(ks_l), jnp.stack(vs_l),
            jnp.stack(pp_l), jnp.stack(ps_l), jnp.stack(cp_l), jnp.stack(cs_l))
```

```python
import functools

import jax
import jax.numpy as jnp
from jax import lax
from jax.experimental import pallas as pl
from jax.experimental.pallas import tpu as pltpu

F32 = jnp.float32
BF16 = jnp.bfloat16

D_MODEL = 2048
DEPTH = 4
PAST_LEN = 1024
CHUNK = 64
N_MIXERS = 3

N_HEADS = 32
N_KV_HEADS = 4
HEAD_DIM = 64
GROUP = N_HEADS // N_KV_HEADS
Q_DIM = N_HEADS * HEAD_DIM
KV_DIM = N_KV_HEADS * HEAD_DIM
QKV_DIM = Q_DIM + 2 * KV_DIM
WINDOW = 128
ROT_DIM = HEAD_DIM // 4
ROPE_THETA = 500000.0

POOL_WINDOWS = (2, 4, 8, 16)
POOL_GROUP_DIM = D_MODEL // len(POOL_WINDOWS)
POOL_PREFIX = max(POOL_WINDOWS) - 1
POOL_PAD = 16

CONV_WIDTH = 31
CONV_PREFIX = CONV_WIDTH - 1
CONV_PAD = 32

D_FF = 5632
RMS_EPS = 1e-5
LN_EPS = 1e-5

LANES = 128
N_COL_BLOCKS = D_MODEL // LANES
VMEM_LIMIT = 56 * 1024 * 1024


def _params(n_axes):
    return pltpu.CompilerParams(dimension_semantics=("arbitrary",) * n_axes,
                                vmem_limit_bytes=VMEM_LIMIT)


def _resident(shape, layer):
    return pl.BlockSpec((None,) + shape, lambda *_: (layer,) + (0,) * len(shape),
                        pipeline_mode=pl.Buffered(1))


def _rms(x, g):
    return x * lax.rsqrt(jnp.mean(x * x, axis=-1, keepdims=True) + RMS_EPS) * g


def _sigmoid(x):
    return 1.0 / (1.0 + jnp.exp(-x))


def _ffn_kernel(*refs, final_norm, convert_next):
    x_ref, g_ref, wg_ref, wu_ref, wd_ref, gf_ref = refs[:6]
    if convert_next:
        gu32_ref, d32_ref, o_ref, gu16_ref, d16_ref, h_ref = refs[6:]
        gu16_ref[...] = gu32_ref[...].astype(BF16)
        d16_ref[...] = d32_ref[...].astype(BF16)
    else:
        o_ref, h_ref = refs[6:]
    j = pl.program_id(1)

    @pl.when(j == 0)
    def _():
        x = x_ref[...]
        h_ref[...] = _rms(x, g_ref[...]).astype(BF16)
        o_ref[...] = x

    h = h_ref[...]
    a_g = jnp.dot(h, wg_ref[...], preferred_element_type=F32)
    a_u = jnp.dot(h, wu_ref[...], preferred_element_type=F32)
    act = (a_g * _sigmoid(a_g)) * a_u
    o_ref[...] += jnp.dot(act.astype(BF16), wd_ref[...], preferred_element_type=F32)

    if final_norm:
        @pl.when(j == pl.num_programs(1) - 1)
        def _():
            o_ref[...] = _rms(o_ref[...], gf_ref[...])


def _ffn(x, g, w_gate_up, w_down, g_final, *, final_norm, tm, tf=512, next_f32=None):
    t = x.shape[0]
    tm = min(tm, t)
    ni, nj = t // tm, D_FF // tf
    in_specs = [
        pl.BlockSpec((tm, D_MODEL), lambda i, j: (i, 0)),
        pl.BlockSpec((1, D_MODEL), lambda i, j: (0, 0)),
        pl.BlockSpec((D_MODEL, tf), lambda i, j: (0, j)),
        pl.BlockSpec((D_MODEL, tf), lambda i, j: (0, j + nj)),
        pl.BlockSpec((tf, D_MODEL), lambda i, j: (j, 0)),
        pl.BlockSpec((1, D_MODEL), lambda i, j: (0, 0)),
    ]
    out_specs = [pl.BlockSpec((tm, D_MODEL), lambda i, j: (i, 0))]
    out_shape = [jax.ShapeDtypeStruct((t, D_MODEL), F32)]
    args = [x, g, w_gate_up, w_gate_up, w_down, g_final]
    if next_f32 is not None:
        gu32, d32, layer = next_f32
        gu_rows, gu_cols = D_MODEL // ni, 2 * D_FF // nj
        d_rows = D_FF // (ni * nj)
        assert gu_rows * ni == D_MODEL and gu_rows % 16 == 0 and gu_cols % LANES == 0
        assert d_rows * ni * nj == D_FF and d_rows % 16 == 0
        in_specs += [pl.BlockSpec((None, gu_rows, gu_cols), lambda i, j: (layer, i, j)),
                     pl.BlockSpec((None, d_rows, D_MODEL), lambda i, j: (layer, i * nj + j, 0))]
        out_specs += [pl.BlockSpec((gu_rows, gu_cols), lambda i, j: (i, j)),
                      pl.BlockSpec((d_rows, D_MODEL), lambda i, j: (i * nj + j, 0))]
        out_shape += [jax.ShapeDtypeStruct((D_MODEL, 2 * D_FF), BF16),
                      jax.ShapeDtypeStruct((D_FF, D_MODEL), BF16)]
        args += [gu32, d32]
    return pl.pallas_call(
        functools.partial(_ffn_kernel, final_norm=final_norm, convert_next=next_f32 is not None),
        grid=(ni, nj),
        in_specs=in_specs,
        out_specs=out_specs,
        out_shape=out_shape,
        scratch_shapes=[pltpu.VMEM((tm, D_MODEL), BF16)],
        compiler_params=_params(2),
        name="swiglu",
    )(*args)


def _rope_tables(pos):
    inv_freq = ROPE_THETA ** (-jnp.arange(0, ROT_DIM, 2, dtype=F32) / ROT_DIM)
    ang = pos.astype(F32)[:, None] * inv_freq[None, :]
    cos, sin = jnp.cos(ang), jnp.sin(ang)
    half = ROT_DIM // 2
    rest = HEAD_DIM - ROT_DIM
    n = pos.shape[0]
    ones, zeros = jnp.ones((n, rest), F32), jnp.zeros((n, rest), F32)
    zh = jnp.zeros((n, half), F32)
    ca = jnp.concatenate([cos, cos, ones], axis=1)
    sb = jnp.concatenate([-sin, zh, zeros], axis=1)
    sc = jnp.concatenate([zh, sin, zeros], axis=1)
    rep = LANES // HEAD_DIM
    return tuple(jnp.tile(a, (1, rep)) for a in (ca, sb, sc))


def _qkv_kernel(*refs, convert):
    x_ref, g_ref, w_ref, b_ref, ca_ref, sb_ref, sc_ref = refs[:7]
    if convert:
        gu32_ref, d32_ref, q_ref, kv_ref, gu16_ref, d16_ref = refs[7:]
        gu16_ref[...] = gu32_ref[...].astype(BF16)
        d16_ref[...] = d32_ref[...].astype(BF16)
    else:
        q_ref, kv_ref = refs[7:]
    h = _rms(x_ref[...], g_ref[...]).astype(BF16)
    qkv = jnp.dot(h, w_ref[...], preferred_element_type=F32) + b_ref[...]
    ca, sb, sc = ca_ref[...], sb_ref[...], sc_ref[...]
    half = ROT_DIM // 2
    for s in range((Q_DIM + KV_DIM) // LANES):
        blk = qkv[:, s * LANES:(s + 1) * LANES]
        r = (blk * ca + pltpu.roll(blk, LANES - half, 1) * sb + pltpu.roll(blk, half, 1) * sc)
        if s < Q_DIM // LANES:
            q_ref[:, s * LANES:(s + 1) * LANES] = (r * (HEAD_DIM ** -0.5)).astype(BF16)
        else:
            c0 = s * LANES - Q_DIM
            kv_ref[:, c0:c0 + LANES] = r
    kv_ref[:, KV_DIM:] = qkv[:, Q_DIM + KV_DIM:]


def _qkv(x, g, w, b, tables, *, layer, tm, ffn_f32=None):
    t = x.shape[0]
    tm = min(tm, t)
    n = t // tm
    row = lambda i: (i, 0)
    const = lambda i: (0, 0)
    in_specs = [
        pl.BlockSpec((tm, D_MODEL), row),
        pl.BlockSpec((1, D_MODEL), const),
        _resident((D_MODEL, QKV_DIM), layer),
        pl.BlockSpec((1, QKV_DIM), const),
        pl.BlockSpec((tm, LANES), row),
        pl.BlockSpec((tm, LANES), row),
        pl.BlockSpec((tm, LANES), row),
    ]
    out_specs = [pl.BlockSpec((tm, Q_DIM), row), pl.BlockSpec((tm, 2 * KV_DIM), row)]
    out_shape = [jax.ShapeDtypeStruct((t, Q_DIM), BF16),
                 jax.ShapeDtypeStruct((t, 2 * KV_DIM), F32)]
    args = [x, g, w, b, *tables]
    if ffn_f32 is not None:
        gu32, d32, ffn_layer = ffn_f32
        gu_rows, d_rows = D_MODEL // n, D_FF // n
        assert gu_rows * n == D_MODEL and gu_rows % 16 == 0
        assert d_rows * n == D_FF and d_rows % 16 == 0
        in_specs += [pl.BlockSpec((None, gu_rows, 2 * D_FF), lambda i: (ffn_layer, i, 0)),
                     pl.BlockSpec((None, d_rows, D_MODEL), lambda i: (ffn_layer, i, 0))]
        out_specs += [pl.BlockSpec((gu_rows, 2 * D_FF), row), pl.BlockSpec((d_rows, D_MODEL), row)]
        out_shape += [jax.ShapeDtypeStruct((D_MODEL, 2 * D_FF), BF16),
                      jax.ShapeDtypeStruct((D_FF, D_MODEL), BF16)]
        args += [gu32, d32]
    return pl.pallas_call(
        functools.partial(_qkv_kernel, convert=ffn_f32 is not None),
        grid=(n,),
        in_specs=in_specs,
        out_specs=out_specs,
        out_shape=out_shape,
        compiler_params=_params(1),
        name="qkv_rope",
    )(*args)


def _half_masked(blk):
    lo = lax.broadcasted_iota(jnp.int32, blk.shape, 1) < HEAD_DIM
    swapped = pltpu.roll(blk, HEAD_DIM, 1)
    zero = jnp.zeros_like(blk)
    sides = ((jnp.where(lo, blk, zero), jnp.where(lo, zero, swapped)),
             (jnp.where(lo, swapped, zero), jnp.where(lo, zero, blk)))
    return [tuple(x.astype(BF16) for x in side) for side in sides]


def _softmax_block(s, sink, valid):
    if valid is not None:
        s = jnp.where(valid, s, -jnp.inf)
    m = jnp.maximum(jnp.max(s, axis=-1, keepdims=True), sink)
    p = jnp.exp(s - m)
    denom = jnp.sum(p, axis=-1, keepdims=True) + jnp.exp(sink - m)
    return p.astype(BF16), 1.0 / denom


def _attention(q_ref, k_parts, v_parts, sinks_ref, valid, store):
    r = q_ref.shape[0]
    pairs = GROUP // 2
    nt = (((1,), (1,)), ((), ()))

    def gather(parts, m):
        return jnp.concatenate([ref[:, off + m * LANES:off + (m + 1) * LANES]
                                for ref, off in parts], axis=0)

    for m in range(KV_DIM // LANES):
        k_ops = _half_masked(gather(k_parts, m))
        v_ops = _half_masked(gather(v_parts, m))
        for gi in range(LANES // HEAD_DIM):
            p0 = (m * (LANES // HEAD_DIM) + gi) * pairs
            qg = jnp.concatenate([q_ref[:, (p0 + pi) * LANES:(p0 + pi + 1) * LANES]
                                  for pi in range(pairs)], axis=0)
            pv, rcp = [], []
            for par in range(2):
                s = lax.dot_general(qg, k_ops[gi][par], nt, preferred_element_type=F32)
                blocks = [_softmax_block(s[pi * r:(pi + 1) * r], sinks_ref[2 * (p0 + pi) + par],
                                         valid) for pi in range(pairs)]
                p_all = jnp.concatenate([b[0] for b in blocks], axis=0)
                pv.append(jnp.dot(p_all, v_ops[gi][par], preferred_element_type=F32))
                rcp.append([b[1] for b in blocks])
            for pi in range(pairs):
                rows = slice(pi * r, (pi + 1) * r)
                o = pv[0][rows] * rcp[0][pi] + pv[1][rows] * rcp[1][pi]
                store(p0 + pi, o.astype(BF16))


def _attn_prompt_kernel(sinks_ref, x_ref, q_ref, kvp_ref, kvc_ref, wo_ref, bo_ref, o_ref,
                        o_scr, *, tq):
    i = pl.program_id(0)
    n_prev = tq // CHUNK
    cq = lax.broadcasted_iota(jnp.int32, (tq, 2 * tq), 0) // CHUNK
    ck = lax.broadcasted_iota(jnp.int32, (tq, 2 * tq), 1) // CHUNK - n_prev
    valid = (ck <= cq) & (ck >= cq - WINDOW // CHUNK) & ((ck >= 0) | (i > 0))

    def store(pair, val):
        o_scr[:, pair * LANES:(pair + 1) * LANES] = val

    _attention(q_ref, [(kvp_ref, 0), (kvc_ref, 0)], [(kvp_ref, KV_DIM), (kvc_ref, KV_DIM)],
               sinks_ref, valid, store)
    y = jnp.dot(o_scr[...], wo_ref[...], preferred_element_type=F32) + bo_ref[...]
    o_ref[...] = x_ref[...] + y


def _attn_prompt(x, q, kv, sinks, wo, bo, *, layer, tq=128):
    t = x.shape[0]
    assert tq >= WINDOW and t % tq == 0
    row = lambda i: (i, 0)
    const = lambda i: (0, 0)
    return pl.pallas_call(
        functools.partial(_attn_prompt_kernel, tq=tq),
        grid=(t // tq,),
        in_specs=[
            pl.BlockSpec(memory_space=pltpu.SMEM),
            pl.BlockSpec((tq, D_MODEL), row),
            pl.BlockSpec((tq, Q_DIM), row),
            pl.BlockSpec((tq, 2 * KV_DIM), lambda i: (jnp.maximum(i - 1, 0), 0)),
            pl.BlockSpec((tq, 2 * KV_DIM), row),
            _resident((Q_DIM, D_MODEL), layer),
            pl.BlockSpec((1, D_MODEL), const),
        ],
        out_specs=pl.BlockSpec((tq, D_MODEL), row),
        out_shape=jax.ShapeDtypeStruct((t, D_MODEL), F32),
        scratch_shapes=[pltpu.VMEM((tq, Q_DIM), BF16)],
        compiler_params=_params(1),
        name="attn_prompt",
    )(sinks, x, q, kv, kv, wo, bo)


def _attn_sample_kernel(sinks_ref, x_ref, q_ref, kv_ref, ck_ref, cv_ref, wo_ref, bo_ref,
                        o_ref, nk_ref, nv_ref, o_scr, *, ts, wc):
    b = pl.program_id(0)
    nk_ref[0:wc - ts, :] = ck_ref[ts:, :]
    nk_ref[wc - ts:, :] = kv_ref[:, :KV_DIM]
    nv_ref[0:wc - ts, :] = cv_ref[ts:, :]
    nv_ref[wc - ts:, :] = kv_ref[:, KV_DIM:]
    row0 = pl.multiple_of(b * ts, ts)

    def store(pair, val):
        o_scr[pl.ds(row0, ts), pair * LANES:(pair + 1) * LANES] = val

    _attention(q_ref, [(ck_ref, 0), (kv_ref, 0)], [(cv_ref, 0), (kv_ref, KV_DIM)],
               sinks_ref, None, store)

    @pl.when(b == pl.num_programs(0) - 1)
    def _():
        y = jnp.dot(o_scr[...], wo_ref[...], preferred_element_type=F32) + bo_ref[...]
        o_ref[...] = x_ref[...] + y


def _attn_sample(x, q, kv, cache_k, cache_v, sinks, wo, bo, *, layer, nb, ts):
    t = x.shape[0]
    wc = cache_k.shape[1]
    assert ts <= wc and ts % 16 == 0
    row = lambda b: (b, 0)
    const = lambda b: (0, 0)
    batch = lambda b: (b, 0, 0)
    return pl.pallas_call(
        functools.partial(_attn_sample_kernel, ts=ts, wc=wc),
        grid=(nb,),
        in_specs=[
            pl.BlockSpec(memory_space=pltpu.SMEM),
            pl.BlockSpec((t, D_MODEL), const),
            pl.BlockSpec((ts, Q_DIM), row),
            pl.BlockSpec((ts, 2 * KV_DIM), row),
            pl.BlockSpec((None, wc, KV_DIM), batch),
            pl.BlockSpec((None, wc, KV_DIM), batch),
            _resident((Q_DIM, D_MODEL), layer),
            pl.BlockSpec((1, D_MODEL), const),
        ],
        out_specs=[pl.BlockSpec((t, D_MODEL), const),
                   pl.BlockSpec((None, wc, KV_DIM), batch),
                   pl.BlockSpec((None, wc, KV_DIM), batch)],
        out_shape=[jax.ShapeDtypeStruct((t, D_MODEL), F32),
                   jax.ShapeDtypeStruct((nb, wc, KV_DIM), F32),
                   jax.ShapeDtypeStruct((nb, wc, KV_DIM), F32)],
        scratch_shapes=[pltpu.VMEM((t, Q_DIM), BF16)],
        compiler_params=_params(1),
        name="attn_sample",
    )(sinks, x, q, kv, cache_k, cache_v, wo, bo)


def _pool_kernel(x_ref, pre_ref, g_ref, w_ref, sc_ref, o_ref, h_out_ref, ext_ref,
                 *, tm, pos0, rows_per_seq, raw_prefix):
    i = pl.program_id(0)
    x = x_ref[...]
    g = g_ref[...]
    h = _rms(x, g)
    if raw_prefix:
        pre = jnp.where(i > 0, _rms(pre_ref[...], g), 0.0)
    else:
        pre = pre_ref[...]
    ext_ref[0:POOL_PAD, :] = pre
    ext_ref[POOL_PAD:, :] = h
    h_out_ref[...] = h[tm - POOL_PAD:, :]
    row = lax.broadcasted_iota(jnp.int32, (tm, 1), 0)
    pos = pos0 + (i * tm + row) % rows_per_seq
    for gi, w in enumerate(POOL_WINDOWS):
        c0 = gi * POOL_GROUP_DIM
        sl = slice(c0, c0 + POOL_GROUP_DIM)
        acc = h[:, sl]
        for j in range(1, w):
            acc = acc + ext_ref[POOL_PAD - j:POOL_PAD - j + tm, sl]
        cnt = jnp.minimum(pos + 1, w).astype(F32)
        pooled = acc / cnt - h[:, sl]
        mixed = jnp.dot(pooled.astype(BF16), w_ref[gi], preferred_element_type=F32)
        o_ref[:, sl] = x[:, sl] + mixed * sc_ref[:, sl]


def _pool(x, prefix, g, w_group, scale, *, layer, tm, pos0, rows_per_seq, raw_prefix):
    t = x.shape[0]
    tm = min(tm, t)
    nblk = tm // POOL_PAD
    if raw_prefix:
        pre_map = lambda i: (jnp.maximum(i * nblk - 1, 0), 0)
    else:
        pre_map = lambda i: (i, 0)
    row = lambda i: (i, 0)
    const = lambda i: (0, 0)
    return pl.pallas_call(
        functools.partial(_pool_kernel, tm=tm, pos0=pos0, rows_per_seq=rows_per_seq,
                          raw_prefix=raw_prefix),
        grid=(t // tm,),
        in_specs=[
            pl.BlockSpec((tm, D_MODEL), row),
            pl.BlockSpec((POOL_PAD, D_MODEL), pre_map),
            pl.BlockSpec((1, D_MODEL), const),
            _resident((len(POOL_WINDOWS), POOL_GROUP_DIM, POOL_GROUP_DIM), layer),
            pl.BlockSpec((1, D_MODEL), const),
        ],
        out_specs=[pl.BlockSpec((tm, D_MODEL), row), pl.BlockSpec((POOL_PAD, D_MODEL), row)],
        out_shape=[jax.ShapeDtypeStruct((t, D_MODEL), F32),
                   jax.ShapeDtypeStruct((t // tm * POOL_PAD, D_MODEL), F32)],
        scratch_shapes=[pltpu.VMEM((tm + POOL_PAD, D_MODEL), F32)],
        compiler_params=_params(1),
        name="pool_mix",
    )(x, prefix, g, w_group, scale)


def _glu(x_ref, g_ref, w1_ref, b1_ref):
    h = _rms(x_ref[...], g_ref[...]).astype(BF16)
    a = jnp.dot(h, w1_ref[...], preferred_element_type=F32) + b1_ref[...]
    return a[:, :D_MODEL] * _sigmoid(a[:, D_MODEL:])


def _dwconv_rows(ext_ref, wdw_ref, c_ref, *, n_rows, ext_row0, out_row0, rc):
    def col_block(cb, carry):
        for r0 in range(0, n_rows, rc):
            acc = jnp.zeros((rc, LANES), F32)
            for j in range(CONV_WIDTH):
                start = ext_row0 + r0 + j
                acc = acc + ext_ref[cb, start:start + rc, :] * wdw_ref[cb, j:j + 1, :]
            c_ref[cb, out_row0 + r0:out_row0 + r0 + rc, :] = acc
        return carry

    lax.fori_loop(0, N_COL_BLOCKS, col_block, 0)


def _conv_tail(x_ref, c_ref, bdw_ref, lng_ref, lnb_ref, w2_ref, b2_ref, o_ref):
    c = jnp.concatenate([c_ref[cb] for cb in range(N_COL_BLOCKS)], axis=1) + bdw_ref[...]
    mu = jnp.mean(c, axis=-1, keepdims=True)
    d = c - mu
    var = jnp.mean(d * d, axis=-1, keepdims=True)
    y = d * lax.rsqrt(var + LN_EPS) * lng_ref[...] + lnb_ref[...]
    z = y * _sigmoid(y)
    out = jnp.dot(z.astype(BF16), w2_ref[...], preferred_element_type=F32) + b2_ref[...]
    o_ref[...] = x_ref[...] + out


def _conv_prompt_kernel(x_ref, g_ref, w1_ref, b1_ref, wdw_ref, bdw_ref, lng_ref, lnb_ref,
                        w2_ref, b2_ref, o_ref, st_ref, ext_ref, c_ref, *, tm):
    i = pl.program_id(0)

    @pl.when(i == 0)
    def _():
        ext_ref[:, 0:CONV_PAD, :] = jnp.zeros((N_COL_BLOCKS, CONV_PAD, LANES), F32)

    u = _glu(x_ref, g_ref, w1_ref, b1_ref)
    for cb in range(N_COL_BLOCKS):
        ext_ref[cb, CONV_PAD:, :] = u[:, cb * LANES:(cb + 1) * LANES]
    _dwconv_rows(ext_ref, wdw_ref, c_ref, n_rows=tm, ext_row0=CONV_PAD - CONV_PREFIX,
                 out_row0=0, rc=64)
    _conv_tail(x_ref, c_ref, bdw_ref, lng_ref, lnb_ref, w2_ref, b2_ref, o_ref)
    for cb in range(N_COL_BLOCKS):
        tail = ext_ref[cb, tm:tm + CONV_PAD, :]
        ext_ref[cb, 0:CONV_PAD, :] = tail
        st_ref[:, cb * LANES:(cb + 1) * LANES] = tail


def _conv_weight_specs(const, layer):
    return [
        pl.BlockSpec((1, D_MODEL), const),
        _resident((D_MODEL, 2 * D_MODEL), layer),
        pl.BlockSpec((1, 2 * D_MODEL), const),
        pl.BlockSpec((N_COL_BLOCKS, CONV_PAD, LANES), lambda i: (0, 0, 0)),
        pl.BlockSpec((1, D_MODEL), const),
        pl.BlockSpec((1, D_MODEL), const),
        pl.BlockSpec((1, D_MODEL), const),
        _resident((D_MODEL, D_MODEL), layer),
        pl.BlockSpec((1, D_MODEL), const),
    ]


def _conv_prompt(x, weights, *, layer, tm):
    t = x.shape[0]
    tm = min(tm, t)
    assert tm >= CONV_PAD
    row = lambda i: (i, 0)
    const = lambda i: (0, 0)
    return pl.pallas_call(
        functools.partial(_conv_prompt_kernel, tm=tm),
        grid=(t // tm,),
        in_specs=[pl.BlockSpec((tm, D_MODEL), row)] + _conv_weight_specs(const, layer),
        out_specs=[pl.BlockSpec((tm, D_MODEL), row), pl.BlockSpec((CONV_PAD, D_MODEL), const)],
        out_shape=[jax.ShapeDtypeStruct((t, D_MODEL), F32),
                   jax.ShapeDtypeStruct((CONV_PAD, D_MODEL), F32)],
        scratch_shapes=[pltpu.VMEM((N_COL_BLOCKS, tm + CONV_PAD, LANES), F32),
                        pltpu.VMEM((N_COL_BLOCKS, tm, LANES), F32)],
        compiler_params=_params(1),
        name="conv_prompt",
    )(x, *weights)


def _conv_sample_kernel(x_ref, st_in_ref, g_ref, w1_ref, b1_ref, wdw_ref, bdw_ref, lng_ref,
                        lnb_ref, w2_ref, b2_ref, o_ref, st_ref, ext_ref, c_ref, *, nb, ts):
    u = _glu(x_ref, g_ref, w1_ref, b1_ref)
    seg = CONV_PAD + ts
    for b in range(nb):
        for cb in range(N_COL_BLOCKS):
            cols = slice(cb * LANES, (cb + 1) * LANES)
            ext_ref[cb, b * seg:b * seg + CONV_PAD, :] = st_in_ref[b, :, cols]
            ext_ref[cb, b * seg + CONV_PAD:(b + 1) * seg, :] = u[b * ts:(b + 1) * ts, cols]
    for b in range(nb):
        _dwconv_rows(ext_ref, wdw_ref, c_ref, n_rows=ts,
                     ext_row0=b * seg + CONV_PAD - CONV_PREFIX, out_row0=b * ts, rc=ts)
    _conv_tail(x_ref, c_ref, bdw_ref, lng_ref, lnb_ref, w2_ref, b2_ref, o_ref)
    for b in range(nb):
        for cb in range(N_COL_BLOCKS):
            st_ref[b, :, cb * LANES:(cb + 1) * LANES] = ext_ref[cb, b * seg + ts:(b + 1) * seg, :]


def _conv_sample(x, state, weights, *, layer, nb, ts):
    t = x.shape[0]
    assert ts % 8 == 0
    const = lambda i: (0, 0)
    const3 = lambda i: (0, 0, 0)
    return pl.pallas_call(
        functools.partial(_conv_sample_kernel, nb=nb, ts=ts),
        grid=(1,),
        in_specs=[pl.BlockSpec((t, D_MODEL), const),
                  pl.BlockSpec((nb, CONV_PAD, D_MODEL), const3)] + _conv_weight_specs(const, layer),
        out_specs=[pl.BlockSpec((t, D_MODEL), const),
                   pl.BlockSpec((nb, CONV_PAD, D_MODEL), const3)],
        out_shape=[jax.ShapeDtypeStruct((t, D_MODEL), F32),
                   jax.ShapeDtypeStruct((nb, CONV_PAD, D_MODEL), F32)],
        scratch_shapes=[pltpu.VMEM((N_COL_BLOCKS, nb * (CONV_PAD + ts), LANES), F32),
                        pltpu.VMEM((N_COL_BLOCKS, t, LANES), F32)],
        compiler_params=_params(1),
        name="conv_sample",
    )(x, state, *weights)


def kernel(x_prompt, x_sample, cache_k, cache_v, state_pool, state_conv, norm_mix, norm_ffn, norm_final, a_w_qkv, a_b_qkv, a_sinks, a_w_o, a_b_o, b_w_group, b_scale, c_w_pw1, c_b_pw1, c_w_dw, c_b_dw, c_ln_g, c_ln_b, c_w_pw2, c_b_pw2, f_w_gate_up, f_w_down):
    bp, seq, _ = x_prompt.shape
    nb, ts, _ = x_sample.shape
    assert bp == 1 and seq % CHUNK == 0
    wc = cache_k.shape[2]
    xp = x_prompt.reshape(seq, D_MODEL)
    xs = x_sample.reshape(nb * ts, D_MODEL)
    row = lambda a: a.reshape(1, -1)

    tables_p = _rope_tables(jnp.arange(seq))
    tables_s = tuple(jnp.tile(a, (nb, 1)) for a in _rope_tables(PAST_LEN + jnp.arange(ts)))
    g_final = row(norm_final)

    w_qkv, w_o = a_w_qkv.astype(BF16), a_w_o.astype(BF16)
    w_g = b_w_group.astype(BF16)
    w_pw1, w_pw2 = c_w_pw1.astype(BF16), c_w_pw2.astype(BF16)
    w_gu = w_d = None

    kp_l, vp_l, ks_l, vs_l = [], [], [], []
    pp_l, ps_l, cp_l, cs_l = [], [], [], []
    for i in range(DEPTH):
        kind, j = i % N_MIXERS, i // N_MIXERS
        g_mix = row(norm_mix[i])
        if kind == 0:
            b_qkv, b_o = row(a_b_qkv[j]), row(a_b_o[j])
            if i == 0:
                q, kv, w_gu, w_d = _qkv(xp, g_mix, w_qkv, b_qkv, tables_p, layer=j, tm=512,
                                        ffn_f32=(f_w_gate_up, f_w_down, 0))
            else:
                q, kv = _qkv(xp, g_mix, w_qkv, b_qkv, tables_p, layer=j, tm=512)
            xp = _attn_prompt(xp, q, kv, a_sinks[j], w_o, b_o, layer=j)
            keep = min(WINDOW, seq)
            kp_l.append(kv[seq - keep:, :KV_DIM].reshape(1, keep, N_KV_HEADS, HEAD_DIM))
            vp_l.append(kv[seq - keep:, KV_DIM:].reshape(1, keep, N_KV_HEADS, HEAD_DIM))
            q, kv = _qkv(xs, g_mix, w_qkv, b_qkv, tables_s, layer=j, tm=nb * ts)
            xs, nk, nv = _attn_sample(xs, q, kv, cache_k[j].reshape(nb, wc, KV_DIM),
                                      cache_v[j].reshape(nb, wc, KV_DIM), a_sinks[j], w_o, b_o,
                                      layer=j, nb=nb, ts=ts)
            ks_l.append(nk.reshape(nb, wc, N_KV_HEADS, HEAD_DIM))
            vs_l.append(nv.reshape(nb, wc, N_KV_HEADS, HEAD_DIM))
        elif kind == 1:
            sc = row(b_scale[j])
            xp, hp = _pool(xp, xp, g_mix, w_g, sc, layer=j, tm=256, pos0=0, rows_per_seq=seq,
                           raw_prefix=True)
            pp_l.append(hp[-POOL_PREFIX:].reshape(1, POOL_PREFIX, D_MODEL))
            assert ts == POOL_PAD
            pre = jnp.pad(state_pool[j], ((0, 0), (POOL_PAD - POOL_PREFIX, 0), (0, 0)))
            xs, hs = _pool(xs, pre.reshape(nb * POOL_PAD, D_MODEL), g_mix, w_g, sc, layer=j,
                           tm=ts, pos0=PAST_LEN, rows_per_seq=ts, raw_prefix=False)
            ext = jnp.concatenate([state_pool[j], hs.reshape(nb, ts, D_MODEL)], axis=1)
            ps_l.append(ext[:, -POOL_PREFIX:])
        else:
            w_dw = jnp.pad(c_w_dw[j], ((0, CONV_PAD - CONV_WIDTH), (0, 0)))
            w_dw = w_dw.reshape(CONV_PAD, N_COL_BLOCKS, LANES).transpose(1, 0, 2)
            weights = (g_mix, w_pw1, row(c_b_pw1[j]), w_dw, row(c_b_dw[j]),
                       row(c_ln_g[j]), row(c_ln_b[j]), w_pw2, row(c_b_pw2[j]))
            xp, st = _conv_prompt(xp, weights, layer=j, tm=256)
            cp_l.append(st[-CONV_PREFIX:].reshape(1, CONV_PREFIX, D_MODEL))
            pre = jnp.pad(state_conv[j], ((0, 0), (CONV_PAD - CONV_PREFIX, 0), (0, 0)))
            xs, st = _conv_sample(xs, pre, weights, layer=j, nb=nb, ts=ts)
            cs_l.append(st[:, -CONV_PREFIX:])
        last = i == DEPTH - 1
        g_ffn = row(norm_ffn[i])
        xs = _ffn(xs, g_ffn, w_gu, w_d, g_final, final_norm=last, tm=256)[0]
        if last:
            xp = _ffn(xp, g_ffn, w_gu, w_d, g_final, final_norm=True, tm=512)[0]
        else:
            xp, w_gu, w_d = _ffn(xp, g_ffn, w_gu, w_d, g_final, final_norm=False, tm=512,
                                 next_f32=(f_w_gate_up, f_w_down, i + 1))
    return (xp.reshape(1, seq, D_MODEL), xs.reshape(nb, ts, D_MODEL),
            jnp.stack(kp_l), jnp.stack(vp_l), jnp.stack(ks_l), jnp.stack(vs_l),
            jnp.stack(pp_l), jnp.stack(ps_l), jnp.stack(cp_l), jnp.stack(cs_l))
```

```python
import functools

import jax
import jax.numpy as jnp
from jax import lax
from jax.experimental import pallas as pl
from jax.experimental.pallas import tpu as pltpu

F32 = jnp.float32
BF16 = jnp.bfloat16

D_MODEL = 2048
DEPTH = 4
PAST_LEN = 1024
CHUNK = 64
N_MIXERS = 3

N_HEADS = 32
N_KV_HEADS = 4
HEAD_DIM = 64
GROUP = N_HEADS // N_KV_HEADS
Q_DIM = N_HEADS * HEAD_DIM
KV_DIM = N_KV_HEADS * HEAD_DIM
QKV_DIM = Q_DIM + 2 * KV_DIM
WINDOW = 128
ROT_DIM = HEAD_DIM // 4
ROPE_THETA = 500000.0

POOL_WINDOWS = (2, 4, 8, 16)
POOL_GROUP_DIM = D_MODEL // len(POOL_WINDOWS)
POOL_PREFIX = max(POOL_WINDOWS) - 1
POOL_PAD = 16

CONV_WIDTH = 31
CONV_PREFIX = CONV_WIDTH - 1
CONV_PAD = 32

D_FF = 5632
FF_TILE = 512
RMS_EPS = 1e-5
LN_EPS = 1e-5

LANES = 128
N_COL_BLOCKS = D_MODEL // LANES
VMEM_LIMIT = 56 * 1024 * 1024


def _params(n_axes):
    return pltpu.CompilerParams(dimension_semantics=("arbitrary",) * n_axes,
                                vmem_limit_bytes=VMEM_LIMIT)


def _resident(shape, layer):
    return pl.BlockSpec((None,) + shape, lambda *_: (layer,) + (0,) * len(shape),
                        pipeline_mode=pl.Buffered(1))


def _rms(x, g):
    return x * lax.rsqrt(jnp.mean(x * x, axis=-1, keepdims=True) + RMS_EPS) * g


def _sigmoid(x):
    return 1.0 / (1.0 + jnp.exp(-x))


def _cast_gate_up(gu32_ref, gu16_ref):
    for j in range(D_FF // FF_TILE):
        lo, hi = j * FF_TILE, (j + 1) * FF_TILE
        gu16_ref[:, 2 * lo:2 * lo + FF_TILE] = gu32_ref[:, lo:hi].astype(BF16)
        gu16_ref[:, 2 * lo + FF_TILE:2 * hi] = gu32_ref[:, D_FF + lo:D_FF + hi].astype(BF16)


def _ffn_weight_cast_specs(n_gu, n_d, layer, gu_map, d_map):
    gu_rows, d_rows = D_MODEL // n_gu, D_FF // n_d
    assert gu_rows * n_gu == D_MODEL and gu_rows % 16 == 0
    assert d_rows * n_d == D_FF and d_rows % 16 == 0
    in_specs = [pl.BlockSpec((None, gu_rows, 2 * D_FF), lambda *ij: (layer, gu_map(*ij), 0)),
                pl.BlockSpec((None, d_rows, D_MODEL), lambda *ij: (layer, d_map(*ij), 0))]
    out_specs = [pl.BlockSpec((gu_rows, 2 * D_FF), lambda *ij: (gu_map(*ij), 0)),
                 pl.BlockSpec((d_rows, D_MODEL), lambda *ij: (d_map(*ij), 0))]
    out_shape = [jax.ShapeDtypeStruct((D_MODEL, 2 * D_FF), BF16),
                 jax.ShapeDtypeStruct((D_FF, D_MODEL), BF16)]
    return in_specs, out_specs, out_shape


def _half_step(nj):
    return (nj + 1) // 2


def _ffn_kernel(*refs, final_norm, convert_next):
    x_ref, g_ref, wgu_ref, wd_ref, gf_ref = refs[:5]
    if convert_next:
        gu32_ref, d32_ref, o_ref, gu16_ref, d16_ref, h_ref = refs[5:]
    else:
        o_ref, h_ref = refs[5:]
    j = pl.program_id(1)

    @pl.when(j == 0)
    def _():
        x = x_ref[...]
        h_ref[...] = _rms(x, g_ref[...]).astype(BF16)
        o_ref[...] = x
        if convert_next:
            d16_ref[...] = d32_ref[...].astype(BF16)

    if convert_next:
        @pl.when((j == 0) | (j == _half_step(pl.num_programs(1))))
        def _():
            _cast_gate_up(gu32_ref, gu16_ref)

    a = jnp.dot(h_ref[...], wgu_ref[...], preferred_element_type=F32)
    a_g, a_u = a[:, :FF_TILE], a[:, FF_TILE:]
    act = (a_g * _sigmoid(a_g)) * a_u
    o_ref[...] += jnp.dot(act.astype(BF16), wd_ref[...], preferred_element_type=F32)

    if final_norm:
        @pl.when(j == pl.num_programs(1) - 1)
        def _():
            o_ref[...] = _rms(o_ref[...], gf_ref[...])


def _ffn(x, g, w_gate_up, w_down, g_final, *, final_norm, tm, next_f32=None):
    t = x.shape[0]
    tm = min(tm, t)
    ni, nj = t // tm, D_FF // FF_TILE
    in_specs = [
        pl.BlockSpec((tm, D_MODEL), lambda i, j: (i, 0)),
        pl.BlockSpec((1, D_MODEL), lambda i, j: (0, 0)),
        pl.BlockSpec((D_MODEL, 2 * FF_TILE), lambda i, j: (0, j)),
        pl.BlockSpec((FF_TILE, D_MODEL), lambda i, j: (j, 0)),
        pl.BlockSpec((1, D_MODEL), lambda i, j: (0, 0)),
    ]
    out_specs = [pl.BlockSpec((tm, D_MODEL), lambda i, j: (i, 0))]
    out_shape = [jax.ShapeDtypeStruct((t, D_MODEL), F32)]
    args = [x, g, w_gate_up, w_down, g_final]
    if next_f32 is not None:
        gu32, d32, layer = next_f32
        cast_in, cast_out, cast_shape = _ffn_weight_cast_specs(
            2 * ni, ni, layer, lambda i, j: 2 * i + (j >= _half_step(nj)).astype(jnp.int32),
            lambda i, j: i)
        in_specs += cast_in
        out_specs += cast_out
        out_shape += cast_shape
        args += [gu32, d32]
    return pl.pallas_call(
        functools.partial(_ffn_kernel, final_norm=final_norm, convert_next=next_f32 is not None),
        grid=(ni, nj),
        in_specs=in_specs,
        out_specs=out_specs,
        out_shape=out_shape,
        scratch_shapes=[pltpu.VMEM((tm, D_MODEL), BF16)],
        compiler_params=_params(2),
        name="swiglu",
    )(*args)


def _rope_tables(pos):
    inv_freq = ROPE_THETA ** (-jnp.arange(0, ROT_DIM, 2, dtype=F32) / ROT_DIM)
    ang = pos.astype(F32)[:, None] * inv_freq[None, :]
    cos, sin = jnp.cos(ang), jnp.sin(ang)
    half = ROT_DIM // 2
    rest = HEAD_DIM - ROT_DIM
    n = pos.shape[0]
    ones, zeros = jnp.ones((n, rest), F32), jnp.zeros((n, rest), F32)
    zh = jnp.zeros((n, half), F32)
    ca = jnp.concatenate([cos, cos, ones], axis=1)
    sb = jnp.concatenate([-sin, zh, zeros], axis=1)
    sc = jnp.concatenate([zh, sin, zeros], axis=1)
    rep = LANES // HEAD_DIM
    return tuple(jnp.tile(a, (1, rep)) for a in (ca, sb, sc))


def _qkv_kernel(*refs, convert):
    x_ref, g_ref, w_ref, b_ref, ca_ref, sb_ref, sc_ref = refs[:7]
    if convert:
        gu32_ref, d32_ref, q_ref, kv_ref, gu16_ref, d16_ref = refs[7:]
        _cast_gate_up(gu32_ref, gu16_ref)
        d16_ref[...] = d32_ref[...].astype(BF16)
    else:
        q_ref, kv_ref = refs[7:]
    h = _rms(x_ref[...], g_ref[...]).astype(BF16)
    qkv = jnp.dot(h, w_ref[...], preferred_element_type=F32) + b_ref[...]
    ca, sb, sc = ca_ref[...], sb_ref[...], sc_ref[...]
    half = ROT_DIM // 2
    for s in range((Q_DIM + KV_DIM) // LANES):
        blk = qkv[:, s * LANES:(s + 1) * LANES]
        r = (blk * ca + pltpu.roll(blk, LANES - half, 1) * sb + pltpu.roll(blk, half, 1) * sc)
        if s < Q_DIM // LANES:
            q_ref[:, s * LANES:(s + 1) * LANES] = (r * (HEAD_DIM ** -0.5)).astype(BF16)
        else:
            c0 = s * LANES - Q_DIM
            kv_ref[:, c0:c0 + LANES] = r
    kv_ref[:, KV_DIM:] = qkv[:, Q_DIM + KV_DIM:]


def _qkv(x, g, w, b, tables, *, layer, tm, ffn_f32=None):
    t = x.shape[0]
    tm = min(tm, t)
    n = t // tm
    row = lambda i: (i, 0)
    const = lambda i: (0, 0)
    in_specs = [
        pl.BlockSpec((tm, D_MODEL), row),
        pl.BlockSpec((1, D_MODEL), const),
        _resident((D_MODEL, QKV_DIM), layer),
        pl.BlockSpec((1, QKV_DIM), const),
        pl.BlockSpec((tm, LANES), row),
        pl.BlockSpec((tm, LANES), row),
        pl.BlockSpec((tm, LANES), row),
    ]
    out_specs = [pl.BlockSpec((tm, Q_DIM), row), pl.BlockSpec((tm, 2 * KV_DIM), row)]
    out_shape = [jax.ShapeDtypeStruct((t, Q_DIM), BF16),
                 jax.ShapeDtypeStruct((t, 2 * KV_DIM), F32)]
    args = [x, g, w, b, *tables]
    if ffn_f32 is not None:
        gu32, d32, ffn_layer = ffn_f32
        cast_in, cast_out, cast_shape = _ffn_weight_cast_specs(n, n, ffn_layer, lambda i: i,
                                                               lambda i: i)
        in_specs += cast_in
        out_specs += cast_out
        out_shape += cast_shape
        args += [gu32, d32]
    return pl.pallas_call(
        functools.partial(_qkv_kernel, convert=ffn_f32 is not None),
        grid=(n,),
        in_specs=in_specs,
        out_specs=out_specs,
        out_shape=out_shape,
        compiler_params=_params(1),
        name="qkv_rope",
    )(*args)


def _half_masked(blk):
    lo = lax.broadcasted_iota(jnp.int32, blk.shape, 1) < HEAD_DIM
    swapped = pltpu.roll(blk, HEAD_DIM, 1)
    zero = jnp.zeros_like(blk)
    sides = ((jnp.where(lo, blk, zero), jnp.where(lo, zero, swapped)),
             (jnp.where(lo, swapped, zero), jnp.where(lo, zero, blk)))
    return [tuple(x.astype(BF16) for x in side) for side in sides]


def _softmax_block(s, sink, valid):
    if valid is not None:
        s = jnp.where(valid, s, -jnp.inf)
    m = jnp.maximum(jnp.max(s, axis=-1, keepdims=True), sink)
    p = jnp.exp(s - m)
    denom = jnp.sum(p, axis=-1, keepdims=True) + jnp.exp(sink - m)
    return p.astype(BF16), 1.0 / denom


def _attention(q_ref, k_parts, v_parts, sinks_ref, valid, store):
    r = q_ref.shape[0]
    pairs = GROUP // 2
    nt = (((1,), (1,)), ((), ()))

    def gather(parts, m):
        return jnp.concatenate([ref[:, off + m * LANES:off + (m + 1) * LANES]
                                for ref, off in parts], axis=0)

    for m in range(KV_DIM // LANES):
        k_ops = _half_masked(gather(k_parts, m))
        v_ops = _half_masked(gather(v_parts, m))
        for gi in range(LANES // HEAD_DIM):
            p0 = (m * (LANES // HEAD_DIM) + gi) * pairs
            qg = jnp.concatenate([q_ref[:, (p0 + pi) * LANES:(p0 + pi + 1) * LANES]
                                  for pi in range(pairs)], axis=0)
            pv, rcp = [], []
            for par in range(2):
                s = lax.dot_general(qg, k_ops[gi][par], nt, preferred_element_type=F32)
                blocks = [_softmax_block(s[pi * r:(pi + 1) * r], sinks_ref[2 * (p0 + pi) + par],
                                         valid) for pi in range(pairs)]
                p_all = jnp.concatenate([b[0] for b in blocks], axis=0)
                pv.append(jnp.dot(p_all, v_ops[gi][par], preferred_element_type=F32))
                rcp.append([b[1] for b in blocks])
            for pi in range(pairs):
                rows = slice(pi * r, (pi + 1) * r)
                o = pv[0][rows] * rcp[0][pi] + pv[1][rows] * rcp[1][pi]
                store(p0 + pi, o.astype(BF16))


def _attn_prompt_kernel(sinks_ref, x_ref, q_ref, kvp_ref, kvc_ref, wo_ref, bo_ref, o_ref,
                        o_scr, *, tq):
    i = pl.program_id(0)
    n_prev = tq // CHUNK
    cq = lax.broadcasted_iota(jnp.int32, (tq, 2 * tq), 0) // CHUNK
    ck = lax.broadcasted_iota(jnp.int32, (tq, 2 * tq), 1) // CHUNK - n_prev
    valid = (ck <= cq) & (ck >= cq - WINDOW // CHUNK) & ((ck >= 0) | (i > 0))

    def store(pair, val):
        o_scr[:, pair * LANES:(pair + 1) * LANES] = val

    _attention(q_ref, [(kvp_ref, 0), (kvc_ref, 0)], [(kvp_ref, KV_DIM), (kvc_ref, KV_DIM)],
               sinks_ref, valid, store)
    y = jnp.dot(o_scr[...], wo_ref[...], preferred_element_type=F32) + bo_ref[...]
    o_ref[...] = x_ref[...] + y


def _attn_prompt(x, q, kv, sinks, wo, bo, *, layer, tq=128):
    t = x.shape[0]
    assert tq >= WINDOW and t % tq == 0
    row = lambda i: (i, 0)
    const = lambda i: (0, 0)
    return pl.pallas_call(
        functools.partial(_attn_prompt_kernel, tq=tq),
        grid=(t // tq,),
        in_specs=[
            pl.BlockSpec(memory_space=pltpu.SMEM),
            pl.BlockSpec((tq, D_MODEL), row),
            pl.BlockSpec((tq, Q_DIM), row),
            pl.BlockSpec((tq, 2 * KV_DIM), lambda i: (jnp.maximum(i - 1, 0), 0)),
            pl.BlockSpec((tq, 2 * KV_DIM), row),
            _resident((Q_DIM, D_MODEL), layer),
            pl.BlockSpec((1, D_MODEL), const),
        ],
        out_specs=pl.BlockSpec((tq, D_MODEL), row),
        out_shape=jax.ShapeDtypeStruct((t, D_MODEL), F32),
        scratch_shapes=[pltpu.VMEM((tq, Q_DIM), BF16)],
        compiler_params=_params(1),
        name="attn_prompt",
    )(sinks, x, q, kv, kv, wo, bo)


def _attn_sample_kernel(sinks_ref, x_ref, q_ref, kv_ref, ck_ref, cv_ref, wo_ref, bo_ref,
                        o_ref, nk_ref, nv_ref, o_scr, *, ts, wc):
    b = pl.program_id(0)
    nk_ref[0:wc - ts, :] = ck_ref[ts:, :]
    nk_ref[wc - ts:, :] = kv_ref[:, :KV_DIM]
    nv_ref[0:wc - ts, :] = cv_ref[ts:, :]
    nv_ref[wc - ts:, :] = kv_ref[:, KV_DIM:]
    row0 = pl.multiple_of(b * ts, ts)

    def store(pair, val):
        o_scr[pl.ds(row0, ts), pair * LANES:(pair + 1) * LANES] = val

    _attention(q_ref, [(ck_ref, 0), (kv_ref, 0)], [(cv_ref, 0), (kv_ref, KV_DIM)],
               sinks_ref, None, store)

    @pl.when(b == pl.num_programs(0) - 1)
    def _():
        y = jnp.dot(o_scr[...], wo_ref[...], preferred_element_type=F32) + bo_ref[...]
        o_ref[...] = x_ref[...] + y


def _attn_sample(x, q, kv, cache_k, cache_v, sinks, wo, bo, *, layer, nb, ts):
    t = x.shape[0]
    wc = cache_k.shape[1]
    assert ts <= wc and ts % 16 == 0
    row = lambda b: (b, 0)
    const = lambda b: (0, 0)
    batch = lambda b: (b, 0, 0)
    return pl.pallas_call(
        functools.partial(_attn_sample_kernel, ts=ts, wc=wc),
        grid=(nb,),
        in_specs=[
            pl.BlockSpec(memory_space=pltpu.SMEM),
            pl.BlockSpec((t, D_MODEL), const),
            pl.BlockSpec((ts, Q_DIM), row),
            pl.BlockSpec((ts, 2 * KV_DIM), row),
            pl.BlockSpec((None, wc, KV_DIM), batch),
            pl.BlockSpec((None, wc, KV_DIM), batch),
            _resident((Q_DIM, D_MODEL), layer),
            pl.BlockSpec((1, D_MODEL), const),
        ],
        out_specs=[pl.BlockSpec((t, D_MODEL), const),
                   pl.BlockSpec((None, wc, KV_DIM), batch),
                   pl.BlockSpec((None, wc, KV_DIM), batch)],
        out_shape=[jax.ShapeDtypeStruct((t, D_MODEL), F32),
                   jax.ShapeDtypeStruct((nb, wc, KV_DIM), F32),
                   jax.ShapeDtypeStruct((nb, wc, KV_DIM), F32)],
        scratch_shapes=[pltpu.VMEM((t, Q_DIM), BF16)],
        compiler_params=_params(1),
        name="attn_sample",
    )(sinks, x, q, kv, cache_k, cache_v, wo, bo)


def _pool_kernel(x_ref, pre_ref, g_ref, w_ref, sc_ref, o_ref, h_out_ref, ext_ref,
                 *, tm, pos0, rows_per_seq, raw_prefix):
    i = pl.program_id(0)
    x = x_ref[...]
    g = g_ref[...]
    h = _rms(x, g)
    if raw_prefix:
        pre = jnp.where(i > 0, _rms(pre_ref[...], g), 0.0)
    else:
        pre = pre_ref[...]
    ext_ref[0:POOL_PAD, :] = pre
    ext_ref[POOL_PAD:, :] = h
    h_out_ref[...] = h[tm - POOL_PAD:, :]
    row = lax.broadcasted_iota(jnp.int32, (tm, 1), 0)
    pos = pos0 + (i * tm + row) % rows_per_seq
    for gi, w in enumerate(POOL_WINDOWS):
        c0 = gi * POOL_GROUP_DIM
        sl = slice(c0, c0 + POOL_GROUP_DIM)
        acc = h[:, sl]
        for j in range(1, w):
            acc = acc + ext_ref[POOL_PAD - j:POOL_PAD - j + tm, sl]
        cnt = jnp.minimum(pos + 1, w).astype(F32)
        pooled = acc / cnt - h[:, sl]
        mixed = jnp.dot(pooled.astype(BF16), w_ref[gi], preferred_element_type=F32)
        o_ref[:, sl] = x[:, sl] + mixed * sc_ref[:, sl]


def _pool(x, prefix, g, w_group, scale, *, layer, tm, pos0, rows_per_seq, raw_prefix):
    t = x.shape[0]
    tm = min(tm, t)
    nblk = tm // POOL_PAD
    if raw_prefix:
        pre_map = lambda i: (jnp.maximum(i * nblk - 1, 0), 0)
    else:
        pre_map = lambda i: (i, 0)
    row = lambda i: (i, 0)
    const = lambda i: (0, 0)
    return pl.pallas_call(
        functools.partial(_pool_kernel, tm=tm, pos0=pos0, rows_per_seq=rows_per_seq,
                          raw_prefix=raw_prefix),
        grid=(t // tm,),
        in_specs=[
            pl.BlockSpec((tm, D_MODEL), row),
            pl.BlockSpec((POOL_PAD, D_MODEL), pre_map),
            pl.BlockSpec((1, D_MODEL), const),
            _resident((len(POOL_WINDOWS), POOL_GROUP_DIM, POOL_GROUP_DIM), layer),
            pl.BlockSpec((1, D_MODEL), const),
        ],
        out_specs=[pl.BlockSpec((tm, D_MODEL), row), pl.BlockSpec((POOL_PAD, D_MODEL), row)],
        out_shape=[jax.ShapeDtypeStruct((t, D_MODEL), F32),
                   jax.ShapeDtypeStruct((t // tm * POOL_PAD, D_MODEL), F32)],
        scratch_shapes=[pltpu.VMEM((tm + POOL_PAD, D_MODEL), F32)],
        compiler_params=_params(1),
        name="pool_mix",
    )(x, prefix, g, w_group, scale)


def _glu(x_ref, g_ref, w1_ref, b1_ref):
    h = _rms(x_ref[...], g_ref[...]).astype(BF16)
    a = jnp.dot(h, w1_ref[...], preferred_element_type=F32) + b1_ref[...]
    return a[:, :D_MODEL] * _sigmoid(a[:, D_MODEL:])


def _dwconv_rows(ext_ref, wdw_ref, c_ref, *, n_rows, ext_row0, out_row0, rc):
    def col_block(cb, carry):
        for r0 in range(0, n_rows, rc):
            acc = jnp.zeros((rc, LANES), F32)
            for j in range(CONV_WIDTH):
                start = ext_row0 + r0 + j
                acc = acc + ext_ref[cb, start:start + rc, :] * wdw_ref[cb, j:j + 1, :]
            c_ref[cb, out_row0 + r0:out_row0 + r0 + rc, :] = acc
        return carry

    lax.fori_loop(0, N_COL_BLOCKS, col_block, 0)


def _conv_tail(x_ref, c_ref, bdw_ref, lng_ref, lnb_ref, w2_ref, b2_ref, o_ref):
    c = jnp.concatenate([c_ref[cb] for cb in range(N_COL_BLOCKS)], axis=1) + bdw_ref[...]
    mu = jnp.mean(c, axis=-1, keepdims=True)
    d = c - mu
    var = jnp.mean(d * d, axis=-1, keepdims=True)
    y = d * lax.rsqrt(var + LN_EPS) * lng_ref[...] + lnb_ref[...]
    z = y * _sigmoid(y)
    out = jnp.dot(z.astype(BF16), w2_ref[...], preferred_element_type=F32) + b2_ref[...]
    o_ref[...] = x_ref[...] + out


def _conv_prompt_kernel(x_ref, g_ref, w1_ref, b1_ref, wdw_ref, bdw_ref, lng_ref, lnb_ref,
                        w2_ref, b2_ref, o_ref, st_ref, ext_ref, c_ref, *, tm):
    i = pl.program_id(0)

    @pl.when(i == 0)
    def _():
        ext_ref[:, 0:CONV_PAD, :] = jnp.zeros((N_COL_BLOCKS, CONV_PAD, LANES), F32)

    u = _glu(x_ref, g_ref, w1_ref, b1_ref)
    for cb in range(N_COL_BLOCKS):
        ext_ref[cb, CONV_PAD:, :] = u[:, cb * LANES:(cb + 1) * LANES]
    _dwconv_rows(ext_ref, wdw_ref, c_ref, n_rows=tm, ext_row0=CONV_PAD - CONV_PREFIX,
                 out_row0=0, rc=64)
    _conv_tail(x_ref, c_ref, bdw_ref, lng_ref, lnb_ref, w2_ref, b2_ref, o_ref)
    for cb in range(N_COL_BLOCKS):
        tail = ext_ref[cb, tm:tm + CONV_PAD, :]
        ext_ref[cb, 0:CONV_PAD, :] = tail
        st_ref[:, cb * LANES:(cb + 1) * LANES] = tail


def _conv_weight_specs(const, layer):
    return [
        pl.BlockSpec((1, D_MODEL), const),
        _resident((D_MODEL, 2 * D_MODEL), layer),
        pl.BlockSpec((1, 2 * D_MODEL), const),
        pl.BlockSpec((N_COL_BLOCKS, CONV_PAD, LANES), lambda i: (0, 0, 0)),
        pl.BlockSpec((1, D_MODEL), const),
        pl.BlockSpec((1, D_MODEL), const),
        pl.BlockSpec((1, D_MODEL), const),
        _resident((D_MODEL, D_MODEL), layer),
        pl.BlockSpec((1, D_MODEL), const),
    ]


def _conv_prompt(x, weights, *, layer, tm):
    t = x.shape[0]
    tm = min(tm, t)
    assert tm >= CONV_PAD
    row = lambda i: (i, 0)
    const = lambda i: (0, 0)
    return pl.pallas_call(
        functools.partial(_conv_prompt_kernel, tm=tm),
        grid=(t // tm,),
        in_specs=[pl.BlockSpec((tm, D_MODEL), row)] + _conv_weight_specs(const, layer),
        out_specs=[pl.BlockSpec((tm, D_MODEL), row), pl.BlockSpec((CONV_PAD, D_MODEL), const)],
        out_shape=[jax.ShapeDtypeStruct((t, D_MODEL), F32),
                   jax.ShapeDtypeStruct((CONV_PAD, D_MODEL), F32)],
        scratch_shapes=[pltpu.VMEM((N_COL_BLOCKS, tm + CONV_PAD, LANES), F32),
                        pltpu.VMEM((N_COL_BLOCKS, tm, LANES), F32)],
        compiler_params=_params(1),
        name="conv_prompt",
    )(x, *weights)


def _conv_sample_kernel(x_ref, st_in_ref, g_ref, w1_ref, b1_ref, wdw_ref, bdw_ref, lng_ref,
                        lnb_ref, w2_ref, b2_ref, o_ref, st_ref, ext_ref, c_ref, *, nb, ts):
    u = _glu(x_ref, g_ref, w1_ref, b1_ref)
    seg = CONV_PAD + ts
    for b in range(nb):
        for cb in range(N_COL_BLOCKS):
            cols = slice(cb * LANES, (cb + 1) * LANES)
            ext_ref[cb, b * seg:b * seg + CONV_PAD, :] = st_in_ref[b, :, cols]
            ext_ref[cb, b * seg + CONV_PAD:(b + 1) * seg, :] = u[b * ts:(b + 1) * ts, cols]
    for b in range(nb):
        _dwconv_rows(ext_ref, wdw_ref, c_ref, n_rows=ts,
                     ext_row0=b * seg + CONV_PAD - CONV_PREFIX, out_row0=b * ts, rc=ts)
    _conv_tail(x_ref, c_ref, bdw_ref, lng_ref, lnb_ref, w2_ref, b2_ref, o_ref)
    for b in range(nb):
        for cb in range(N_COL_BLOCKS):
            st_ref[b, :, cb * LANES:(cb + 1) * LANES] = ext_ref[cb, b * seg + ts:(b + 1) * seg, :]


def _conv_sample(x, state, weights, *, layer, nb, ts):
    t = x.shape[0]
    assert ts % 8 == 0
    const = lambda i: (0, 0)
    const3 = lambda i: (0, 0, 0)
    return pl.pallas_call(
        functools.partial(_conv_sample_kernel, nb=nb, ts=ts),
        grid=(1,),
        in_specs=[pl.BlockSpec((t, D_MODEL), const),
                  pl.BlockSpec((nb, CONV_PAD, D_MODEL), const3)] + _conv_weight_specs(const, layer),
        out_specs=[pl.BlockSpec((t, D_MODEL), const),
                   pl.BlockSpec((nb, CONV_PAD, D_MODEL), const3)],
        out_shape=[jax.ShapeDtypeStruct((t, D_MODEL), F32),
                   jax.ShapeDtypeStruct((nb, CONV_PAD, D_MODEL), F32)],
        scratch_shapes=[pltpu.VMEM((N_COL_BLOCKS, nb * (CONV_PAD + ts), LANES), F32),
                        pltpu.VMEM((N_COL_BLOCKS, t, LANES), F32)],
        compiler_params=_params(1),
        name="conv_sample",
    )(x, state, *weights)


def kernel(x_prompt, x_sample, cache_k, cache_v, state_pool, state_conv, norm_mix, norm_ffn, norm_final, a_w_qkv, a_b_qkv, a_sinks, a_w_o, a_b_o, b_w_group, b_scale, c_w_pw1, c_b_pw1, c_w_dw, c_b_dw, c_ln_g, c_ln_b, c_w_pw2, c_b_pw2, f_w_gate_up, f_w_down):
    bp, seq, _ = x_prompt.shape
    nb, ts, _ = x_sample.shape
    assert bp == 1 and seq % CHUNK == 0
    wc = cache_k.shape[2]
    xp = x_prompt.reshape(seq, D_MODEL)
    xs = x_sample.reshape(nb * ts, D_MODEL)
    row = lambda a: a.reshape(1, -1)

    tables_p = _rope_tables(jnp.arange(seq))
    tables_s = tuple(jnp.tile(a, (nb, 1)) for a in _rope_tables(PAST_LEN + jnp.arange(ts)))
    g_final = row(norm_final)

    w_qkv, w_o = a_w_qkv.astype(BF16), a_w_o.astype(BF16)
    w_g = b_w_group.astype(BF16)
    w_pw1, w_pw2 = c_w_pw1.astype(BF16), c_w_pw2.astype(BF16)
    w_gu = w_d = None

    kp_l, vp_l, ks_l, vs_l = [], [], [], []
    pp_l, ps_l, cp_l, cs_l = [], [], [], []
    for i in range(DEPTH):
        kind, j = i % N_MIXERS, i // N_MIXERS
        g_mix = row(norm_mix[i])
        if kind == 0:
            b_qkv, b_o = row(a_b_qkv[j]), row(a_b_o[j])
            if i == 0:
                q, kv, w_gu, w_d = _qkv(xp, g_mix, w_qkv, b_qkv, tables_p, layer=j, tm=512,
                                        ffn_f32=(f_w_gate_up, f_w_down, 0))
            else:
                q, kv = _qkv(xp, g_mix, w_qkv, b_qkv, tables_p, layer=j, tm=512)
            xp = _attn_prompt(xp, q, kv, a_sinks[j], w_o, b_o, layer=j)
            keep = min(WINDOW, seq)
            kp_l.append(kv[seq - keep:, :KV_DIM].reshape(1, keep, N_KV_HEADS, HEAD_DIM))
            vp_l.append(kv[seq - keep:, KV_DIM:].reshape(1, keep, N_KV_HEADS, HEAD_DIM))
            q, kv = _qkv(xs, g_mix, w_qkv, b_qkv, tables_s, layer=j, tm=nb * ts)
            xs, nk, nv = _attn_sample(xs, q, kv, cache_k[j].reshape(nb, wc, KV_DIM),
                                      cache_v[j].reshape(nb, wc, KV_DIM), a_sinks[j], w_o, b_o,
                                      layer=j, nb=nb, ts=ts)
            ks_l.append(nk.reshape(nb, wc, N_KV_HEADS, HEAD_DIM))
            vs_l.append(nv.reshape(nb, wc, N_KV_HEADS, HEAD_DIM))
        elif kind == 1:
            sc = row(b_scale[j])
            xp, hp = _pool(xp, xp, g_mix, w_g, sc, layer=j, tm=256, pos0=0, rows_per_seq=seq,
                           raw_prefix=True)
            pp_l.append(hp[-POOL_PREFIX:].reshape(1, POOL_PREFIX, D_MODEL))
            assert ts == POOL_PAD
            pre = jnp.pad(state_pool[j], ((0, 0), (POOL_PAD - POOL_PREFIX, 0), (0, 0)))
            xs, hs = _pool(xs, pre.reshape(nb * POOL_PAD, D_MODEL), g_mix, w_g, sc, layer=j,
                           tm=ts, pos0=PAST_LEN, rows_per_seq=ts, raw_prefix=False)
            ext = jnp.concatenate([state_pool[j], hs.reshape(nb, ts, D_MODEL)], axis=1)
            ps_l.append(ext[:, -POOL_PREFIX:])
        else:
            w_dw = jnp.pad(c_w_dw[j], ((0, CONV_PAD - CONV_WIDTH), (0, 0)))
            w_dw = w_dw.reshape(CONV_PAD, N_COL_BLOCKS, LANES).transpose(1, 0, 2)
            weights = (g_mix, w_pw1, row(c_b_pw1[j]), w_dw, row(c_b_dw[j]),
                       row(c_ln_g[j]), row(c_ln_b[j]), w_pw2, row(c_b_pw2[j]))
            xp, st = _conv_prompt(xp, weights, layer=j, tm=256)
            cp_l.append(st[-CONV_PREFIX:].reshape(1, CONV_PREFIX, D_MODEL))
            pre = jnp.pad(state_conv[j], ((0, 0), (CONV_PAD - CONV_PREFIX, 0), (0, 0)))
            xs, st = _conv_sample(xs, pre, weights, layer=j, nb=nb, ts=ts)
            cs_l.append(st[:, -CONV_PREFIX:])
        last = i == DEPTH - 1
        g_ffn = row(norm_ffn[i])
        xs = _ffn(xs, g_ffn, w_gu, w_d, g_final, final_norm=last, tm=256)[0]
        if last:
            xp = _ffn(xp, g_ffn, w_gu, w_d, g_final, final_norm=True, tm=512)[0]
        else:
            xp, w_gu, w_d = _ffn(xp, g_ffn, w_gu, w_d, g_final, final_norm=False, tm=512,
                                 next_f32=(f_w_gate_up, f_w_down, i + 1))
    return (xp.reshape(1, seq, D_MODEL), xs.reshape(nb, ts, D_MODEL),
            jnp.stack(kp_l), jnp.stack(vp_l), jnp.stack(ks_l), jnp.stack(vs_l),
            jnp.stack(pp_l), jnp.stack(ps_l), jnp.stack(cp_l), jnp.stack(cs_l))
```

```python
import functools

import jax
import jax.numpy as jnp
from jax import lax
from jax.experimental import pallas as pl
from jax.experimental.pallas import tpu as pltpu

F32 = jnp.float32
BF16 = jnp.bfloat16

D_MODEL = 2048
DEPTH = 4
PAST_LEN = 1024
CHUNK = 64
N_MIXERS = 3

N_HEADS = 32
N_KV_HEADS = 4
HEAD_DIM = 64
GROUP = N_HEADS // N_KV_HEADS
Q_DIM = N_HEADS * HEAD_DIM
KV_DIM = N_KV_HEADS * HEAD_DIM
QKV_DIM = Q_DIM + 2 * KV_DIM
WINDOW = 128
ROT_DIM = HEAD_DIM // 4
ROPE_THETA = 500000.0

POOL_WINDOWS = (2, 4, 8, 16)
POOL_GROUP_DIM = D_MODEL // len(POOL_WINDOWS)
POOL_PREFIX = max(POOL_WINDOWS) - 1
POOL_PAD = 16

CONV_WIDTH = 31
CONV_PREFIX = CONV_WIDTH - 1
CONV_PAD = 32

D_FF = 5632
RMS_EPS = 1e-5
LN_EPS = 1e-5

LANES = 128
N_COL_BLOCKS = D_MODEL // LANES
VMEM_LIMIT = 56 * 1024 * 1024


def _params(n_axes):
    return pltpu.CompilerParams(dimension_semantics=("arbitrary",) * n_axes,
                                vmem_limit_bytes=VMEM_LIMIT)


def _resident(shape, layer):
    return pl.BlockSpec((None,) + shape, lambda *_: (layer,) + (0,) * len(shape),
                        pipeline_mode=pl.Buffered(1))


def _rms(x, g):
    return x * lax.rsqrt(jnp.mean(x * x, axis=-1, keepdims=True) + RMS_EPS) * g


def _sigmoid(x):
    return 1.0 / (1.0 + jnp.exp(-x))


def _ffn_kernel(*refs, final_norm, convert_next):
    x_ref, g_ref, wg_ref, wu_ref, wd_ref, gf_ref = refs[:6]
    if convert_next:
        gu32_ref, d32_ref, o_ref, gu16_ref, d16_ref, h_ref = refs[6:]
        gu16_ref[...] = gu32_ref[...].astype(BF16)
        d16_ref[...] = d32_ref[...].astype(BF16)
    else:
        o_ref, h_ref = refs[6:]
    j = pl.program_id(1)

    @pl.when(j == 0)
    def _():
        x = x_ref[...]
        h_ref[...] = _rms(x, g_ref[...]).astype(BF16)
        o_ref[...] = x

    h = h_ref[...]
    a_g = jnp.dot(h, wg_ref[...], preferred_element_type=F32)
    a_u = jnp.dot(h, wu_ref[...], preferred_element_type=F32)
    act = (a_g * _sigmoid(a_g)) * a_u
    o_ref[...] += jnp.dot(act.astype(BF16), wd_ref[...], preferred_element_type=F32)

    if final_norm:
        @pl.when(j == pl.num_programs(1) - 1)
        def _():
            o_ref[...] = _rms(o_ref[...], gf_ref[...])


def _ffn(x, g, w_gate_up, w_down, g_final, *, final_norm, tm, tf=512, next_f32=None):
    t = x.shape[0]
    tm = min(tm, t)
    ni, nj = t // tm, D_FF // tf
    in_specs = [
        pl.BlockSpec((tm, D_MODEL), lambda i, j: (i, 0)),
        pl.BlockSpec((1, D_MODEL), lambda i, j: (0, 0)),
        pl.BlockSpec((D_MODEL, tf), lambda i, j: (0, j)),
        pl.BlockSpec((D_MODEL, tf), lambda i, j: (0, j + nj)),
        pl.BlockSpec((tf, D_MODEL), lambda i, j: (j, 0)),
        pl.BlockSpec((1, D_MODEL), lambda i, j: (0, 0)),
    ]
    out_specs = [pl.BlockSpec((tm, D_MODEL), lambda i, j: (i, 0))]
    out_shape = [jax.ShapeDtypeStruct((t, D_MODEL), F32)]
    args = [x, g, w_gate_up, w_gate_up, w_down, g_final]
    if next_f32 is not None:
        gu32, d32, layer = next_f32
        gu_rows, gu_cols = D_MODEL // ni, 2 * D_FF // nj
        d_rows = D_FF // (ni * nj)
        assert gu_rows * ni == D_MODEL and gu_rows % 16 == 0 and gu_cols % LANES == 0
        assert d_rows * ni * nj == D_FF and d_rows % 16 == 0
        in_specs += [pl.BlockSpec((None, gu_rows, gu_cols), lambda i, j: (layer, i, j)),
                     pl.BlockSpec((None, d_rows, D_MODEL), lambda i, j: (layer, i * nj + j, 0))]
        out_specs += [pl.BlockSpec((gu_rows, gu_cols), lambda i, j: (i, j)),
                      pl.BlockSpec((d_rows, D_MODEL), lambda i, j: (i * nj + j, 0))]
        out_shape += [jax.ShapeDtypeStruct((D_MODEL, 2 * D_FF), BF16),
                      jax.ShapeDtypeStruct((D_FF, D_MODEL), BF16)]
        args += [gu32, d32]
    return pl.pallas_call(
        functools.partial(_ffn_kernel, final_norm=final_norm, convert_next=next_f32 is not None),
        grid=(ni, nj),
        in_specs=in_specs,
        out_specs=out_specs,
        out_shape=out_shape,
        scratch_shapes=[pltpu.VMEM((tm, D_MODEL), BF16)],
        compiler_params=_params(2),
        name="swiglu",
    )(*args)


def _rope_tables(pos):
    inv_freq = ROPE_THETA ** (-jnp.arange(0, ROT_DIM, 2, dtype=F32) / ROT_DIM)
    ang = pos.astype(F32)[:, None] * inv_freq[None, :]
    cos, sin = jnp.cos(ang), jnp.sin(ang)
    half = ROT_DIM // 2
    rest = HEAD_DIM - ROT_DIM
    n = pos.shape[0]
    ones, zeros = jnp.ones((n, rest), F32), jnp.zeros((n, rest), F32)
    zh = jnp.zeros((n, half), F32)
    ca = jnp.concatenate([cos, cos, ones], axis=1)
    sb = jnp.concatenate([-sin, zh, zeros], axis=1)
    sc = jnp.concatenate([zh, sin, zeros], axis=1)
    rep = LANES // HEAD_DIM
    return tuple(jnp.tile(a, (1, rep)) for a in (ca, sb, sc))


def _qkv_kernel(*refs, convert):
    x_ref, g_ref, w_ref, b_ref, ca_ref, sb_ref, sc_ref = refs[:7]
    if convert:
        gu32_ref, d32_ref, q_ref, kv_ref, gu16_ref, d16_ref = refs[7:]
        gu16_ref[...] = gu32_ref[...].astype(BF16)
        d16_ref[...] = d32_ref[...].astype(BF16)
    else:
        q_ref, kv_ref = refs[7:]
    h = _rms(x_ref[...], g_ref[...]).astype(BF16)
    qkv = jnp.dot(h, w_ref[...], preferred_element_type=F32) + b_ref[...]
    ca, sb, sc = ca_ref[...], sb_ref[...], sc_ref[...]
    half = ROT_DIM // 2
    for s in range((Q_DIM + KV_DIM) // LANES):
        blk = qkv[:, s * LANES:(s + 1) * LANES]
        r = (blk * ca + pltpu.roll(blk, LANES - half, 1) * sb + pltpu.roll(blk, half, 1) * sc)
        if s < Q_DIM // LANES:
            q_ref[:, s * LANES:(s + 1) * LANES] = (r * (HEAD_DIM ** -0.5)).astype(BF16)
        else:
            c0 = s * LANES - Q_DIM
            kv_ref[:, c0:c0 + LANES] = r
    kv_ref[:, KV_DIM:] = qkv[:, Q_DIM + KV_DIM:]


def _qkv(x, g, w, b, tables, *, layer, tm, ffn_f32=None):
    t = x.shape[0]
    tm = min(tm, t)
    n = t // tm
    row = lambda i: (i, 0)
    const = lambda i: (0, 0)
    in_specs = [
        pl.BlockSpec((tm, D_MODEL), row),
        pl.BlockSpec((1, D_MODEL), const),
        _resident((D_MODEL, QKV_DIM), layer),
        pl.BlockSpec((1, QKV_DIM), const),
        pl.BlockSpec((tm, LANES), row),
        pl.BlockSpec((tm, LANES), row),
        pl.BlockSpec((tm, LANES), row),
    ]
    out_specs = [pl.BlockSpec((tm, Q_DIM), row), pl.BlockSpec((tm, 2 * KV_DIM), row)]
    out_shape = [jax.ShapeDtypeStruct((t, Q_DIM), BF16),
                 jax.ShapeDtypeStruct((t, 2 * KV_DIM), F32)]
    args = [x, g, w, b, *tables]
    if ffn_f32 is not None:
        gu32, d32, ffn_layer = ffn_f32
        gu_rows, d_rows = D_MODEL // n, D_FF // n
        assert gu_rows * n == D_MODEL and gu_rows % 16 == 0
        assert d_rows * n == D_FF and d_rows % 16 == 0
        in_specs += [pl.BlockSpec((None, gu_rows, 2 * D_FF), lambda i: (ffn_layer, i, 0)),
                     pl.BlockSpec((None, d_rows, D_MODEL), lambda i: (ffn_layer, i, 0))]
        out_specs += [pl.BlockSpec((gu_rows, 2 * D_FF), row), pl.BlockSpec((d_rows, D_MODEL), row)]
        out_shape += [jax.ShapeDtypeStruct((D_MODEL, 2 * D_FF), BF16),
                      jax.ShapeDtypeStruct((D_FF, D_MODEL), BF16)]
        args += [gu32, d32]
    return pl.pallas_call(
        functools.partial(_qkv_kernel, convert=ffn_f32 is not None),
        grid=(n,),
        in_specs=in_specs,
        out_specs=out_specs,
        out_shape=out_shape,
        compiler_params=_params(1),
        name="qkv_rope",
    )(*args)


def _half_masked(blk):
    lo = lax.broadcasted_iota(jnp.int32, blk.shape, 1) < HEAD_DIM
    swapped = pltpu.roll(blk, HEAD_DIM, 1)
    zero = jnp.zeros_like(blk)
    sides = ((jnp.where(lo, blk, zero), jnp.where(lo, zero, swapped)),
             (jnp.where(lo, swapped, zero), jnp.where(lo, zero, blk)))
    return [tuple(x.astype(BF16) for x in side) for side in sides]


def _softmax_block(s, sink, valid):
    if valid is not None:
        s = jnp.where(valid, s, -jnp.inf)
    m = jnp.maximum(jnp.max(s, axis=-1, keepdims=True), sink)
    p = jnp.exp(s - m)
    denom = jnp.sum(p, axis=-1, keepdims=True) + jnp.exp(sink - m)
    return p.astype(BF16), 1.0 / denom


def _attention(q_ref, q_row0, r, k_parts, v_parts, sinks_ref, valid, store):
    pairs = GROUP // 2
    nt = (((1,), (1,)), ((), ()))

    def gather(parts, m):
        return jnp.concatenate([ref[r0:r0 + n, off + m * LANES:off + (m + 1) * LANES]
                                for ref, r0, n, off in parts], axis=0)

    for m in range(KV_DIM // LANES):
        k_ops = _half_masked(gather(k_parts, m))
        v_ops = _half_masked(gather(v_parts, m))
        for gi in range(LANES // HEAD_DIM):
            p0 = (m * (LANES // HEAD_DIM) + gi) * pairs
            qg = jnp.concatenate(
                [q_ref[q_row0:q_row0 + r, (p0 + pi) * LANES:(p0 + pi + 1) * LANES]
                 for pi in range(pairs)], axis=0)
            pv, rcp = [], []
            for par in range(2):
                s = lax.dot_general(qg, k_ops[gi][par], nt, preferred_element_type=F32)
                blocks = [_softmax_block(s[pi * r:(pi + 1) * r], sinks_ref[2 * (p0 + pi) + par],
                                         valid) for pi in range(pairs)]
                p_all = jnp.concatenate([b[0] for b in blocks], axis=0)
                pv.append(jnp.dot(p_all, v_ops[gi][par], preferred_element_type=F32))
                rcp.append([b[1] for b in blocks])
            for pi in range(pairs):
                rows = slice(pi * r, (pi + 1) * r)
                o = pv[0][rows] * rcp[0][pi] + pv[1][rows] * rcp[1][pi]
                store(p0 + pi, o.astype(BF16))


def _attn_prompt_kernel(sinks_ref, x_ref, q_ref, kvp_ref, kvc_ref, wo_ref, bo_ref, o_ref,
                        o_scr, *, tq):
    i = pl.program_id(0)
    sub = WINDOW
    n_prev = sub // CHUNK
    cq = lax.broadcasted_iota(jnp.int32, (sub, 2 * sub), 0) // CHUNK
    ck = lax.broadcasted_iota(jnp.int32, (sub, 2 * sub), 1) // CHUNK - n_prev
    band = (ck <= cq) & (ck >= cq - WINDOW // CHUNK)
    for t0 in range(0, tq, sub):
        if t0 == 0:
            before = kvp_ref, 0
            valid = band & ((ck >= 0) | (i > 0))
        else:
            before = kvc_ref, t0 - sub
            valid = band

        def store(pair, val, t0=t0):
            o_scr[t0:t0 + sub, pair * LANES:(pair + 1) * LANES] = val

        k_parts = [(before[0], before[1], sub, 0), (kvc_ref, t0, sub, 0)]
        v_parts = [(before[0], before[1], sub, KV_DIM), (kvc_ref, t0, sub, KV_DIM)]
        _attention(q_ref, t0, sub, k_parts, v_parts, sinks_ref, valid, store)
    y = jnp.dot(o_scr[...], wo_ref[...], preferred_element_type=F32) + bo_ref[...]
    o_ref[...] = x_ref[...] + y


def _attn_prompt(x, q, kv, sinks, wo, bo, *, layer, tq):
    t = x.shape[0]
    tq = min(tq, t)
    assert tq % WINDOW == 0 and t % tq == 0
    per = tq // WINDOW
    row = lambda i: (i, 0)
    const = lambda i: (0, 0)
    return pl.pallas_call(
        functools.partial(_attn_prompt_kernel, tq=tq),
        grid=(t // tq,),
        in_specs=[
            pl.BlockSpec(memory_space=pltpu.SMEM),
            pl.BlockSpec((tq, D_MODEL), row),
            pl.BlockSpec((tq, Q_DIM), row),
            pl.BlockSpec((WINDOW, 2 * KV_DIM), lambda i: (jnp.maximum(i * per - 1, 0), 0)),
            pl.BlockSpec((tq, 2 * KV_DIM), row),
            _resident((Q_DIM, D_MODEL), layer),
            pl.BlockSpec((1, D_MODEL), const),
        ],
        out_specs=pl.BlockSpec((tq, D_MODEL), row),
        out_shape=jax.ShapeDtypeStruct((t, D_MODEL), F32),
        scratch_shapes=[pltpu.VMEM((tq, Q_DIM), BF16)],
        compiler_params=_params(1),
        name="attn_prompt",
    )(sinks, x, q, kv, kv, wo, bo)


def _attn_sample_kernel(sinks_ref, x_ref, q_ref, kv_ref, ck_ref, cv_ref, wo_ref, bo_ref,
                        o_ref, nk_ref, nv_ref, o_scr, *, ts, wc, per):
    step = pl.program_id(0)
    for bb in range(per):
        r0 = bb * ts
        nk_ref[bb, 0:wc - ts, :] = ck_ref[bb, ts:, :]
        nk_ref[bb, wc - ts:, :] = kv_ref[r0:r0 + ts, :KV_DIM]
        nv_ref[bb, 0:wc - ts, :] = cv_ref[bb, ts:, :]
        nv_ref[bb, wc - ts:, :] = kv_ref[r0:r0 + ts, KV_DIM:]
        row0 = pl.multiple_of((step * per + bb) * ts, ts)

        def store(pair, val, row0=row0):
            o_scr[pl.ds(row0, ts), pair * LANES:(pair + 1) * LANES] = val

        k_parts = [(ck_ref.at[bb], 0, wc, 0), (kv_ref, r0, ts, 0)]
        v_parts = [(cv_ref.at[bb], 0, wc, 0), (kv_ref, r0, ts, KV_DIM)]
        _attention(q_ref, r0, ts, k_parts, v_parts, sinks_ref, None, store)

    @pl.when(step == pl.num_programs(0) - 1)
    def _():
        y = jnp.dot(o_scr[...], wo_ref[...], preferred_element_type=F32) + bo_ref[...]
        o_ref[...] = x_ref[...] + y


def _attn_sample(x, q, kv, cache_k, cache_v, sinks, wo, bo, *, layer, nb, ts, per):
    t = x.shape[0]
    wc = cache_k.shape[1]
    assert ts <= wc and ts % 16 == 0 and nb % per == 0
    row = lambda b: (b, 0)
    const = lambda b: (0, 0)
    batch = lambda b: (b, 0, 0)
    return pl.pallas_call(
        functools.partial(_attn_sample_kernel, ts=ts, wc=wc, per=per),
        grid=(nb // per,),
        in_specs=[
            pl.BlockSpec(memory_space=pltpu.SMEM),
            pl.BlockSpec((t, D_MODEL), const),
            pl.BlockSpec((per * ts, Q_DIM), row),
            pl.BlockSpec((per * ts, 2 * KV_DIM), row),
            pl.BlockSpec((per, wc, KV_DIM), batch),
            pl.BlockSpec((per, wc, KV_DIM), batch),
            _resident((Q_DIM, D_MODEL), layer),
            pl.BlockSpec((1, D_MODEL), const),
        ],
        out_specs=[pl.BlockSpec((t, D_MODEL), const),
                   pl.BlockSpec((per, wc, KV_DIM), batch),
                   pl.BlockSpec((per, wc, KV_DIM), batch)],
        out_shape=[jax.ShapeDtypeStruct((t, D_MODEL), F32),
                   jax.ShapeDtypeStruct((nb, wc, KV_DIM), F32),
                   jax.ShapeDtypeStruct((nb, wc, KV_DIM), F32)],
        scratch_shapes=[pltpu.VMEM((t, Q_DIM), BF16)],
        compiler_params=_params(1),
        name="attn_sample",
    )(sinks, x, q, kv, cache_k, cache_v, wo, bo)


def _pool_kernel(x_ref, pre_ref, g_ref, w_ref, sc_ref, o_ref, h_out_ref, ext_ref,
                 *, tm, pos0, rows_per_seq, raw_prefix):
    i = pl.program_id(0)
    x = x_ref[...]
    g = g_ref[...]
    h = _rms(x, g)
    if raw_prefix:
        pre = jnp.where(i > 0, _rms(pre_ref[...], g), 0.0)
    else:
        pre = pre_ref[...]
    ext_ref[0:POOL_PAD, :] = pre
    ext_ref[POOL_PAD:, :] = h
    h_out_ref[...] = h[tm - POOL_PAD:, :]
    row = lax.broadcasted_iota(jnp.int32, (tm, 1), 0)
    pos = pos0 + (i * tm + row) % rows_per_seq
    for gi, w in enumerate(POOL_WINDOWS):
        c0 = gi * POOL_GROUP_DIM
        sl = slice(c0, c0 + POOL_GROUP_DIM)
        acc = h[:, sl]
        for j in range(1, w):
            acc = acc + ext_ref[POOL_PAD - j:POOL_PAD - j + tm, sl]
        cnt = jnp.minimum(pos + 1, w).astype(F32)
        pooled = acc / cnt - h[:, sl]
        mixed = jnp.dot(pooled.astype(BF16), w_ref[gi], preferred_element_type=F32)
        o_ref[:, sl] = x[:, sl] + mixed * sc_ref[:, sl]


def _pool(x, prefix, g, w_group, scale, *, layer, tm, pos0, rows_per_seq, raw_prefix):
    t = x.shape[0]
    tm = min(tm, t)
    nblk = tm // POOL_PAD
    if raw_prefix:
        pre_map = lambda i: (jnp.maximum(i * nblk - 1, 0), 0)
    else:
        pre_map = lambda i: (i, 0)
    row = lambda i: (i, 0)
    const = lambda i: (0, 0)
    return pl.pallas_call(
        functools.partial(_pool_kernel, tm=tm, pos0=pos0, rows_per_seq=rows_per_seq,
                          raw_prefix=raw_prefix),
        grid=(t // tm,),
        in_specs=[
            pl.BlockSpec((tm, D_MODEL), row),
            pl.BlockSpec((POOL_PAD, D_MODEL), pre_map),
            pl.BlockSpec((1, D_MODEL), const),
            _resident((len(POOL_WINDOWS), POOL_GROUP_DIM, POOL_GROUP_DIM), layer),
            pl.BlockSpec((1, D_MODEL), const),
        ],
        out_specs=[pl.BlockSpec((tm, D_MODEL), row), pl.BlockSpec((POOL_PAD, D_MODEL), row)],
        out_shape=[jax.ShapeDtypeStruct((t, D_MODEL), F32),
                   jax.ShapeDtypeStruct((t // tm * POOL_PAD, D_MODEL), F32)],
        scratch_shapes=[pltpu.VMEM((tm + POOL_PAD, D_MODEL), F32)],
        compiler_params=_params(1),
        name="pool_mix",
    )(x, prefix, g, w_group, scale)


def _glu(x_ref, g_ref, w1_ref, b1_ref):
    h = _rms(x_ref[...], g_ref[...]).astype(BF16)
    a = jnp.dot(h, w1_ref[...], preferred_element_type=F32) + b1_ref[...]
    return a[:, :D_MODEL] * _sigmoid(a[:, D_MODEL:])


def _dwconv_rows(ext_ref, wdw_ref, c_ref, *, n_rows, ext_row0, out_row0, rc):
    def col_block(cb, carry):
        for r0 in range(0, n_rows, rc):
            acc = jnp.zeros((rc, LANES), F32)
            for j in range(CONV_WIDTH):
                start = ext_row0 + r0 + j
                acc = acc + ext_ref[cb, start:start + rc, :] * wdw_ref[cb, j:j + 1, :]
            c_ref[cb, out_row0 + r0:out_row0 + r0 + rc, :] = acc
        return carry

    lax.fori_loop(0, N_COL_BLOCKS, col_block, 0)


def _conv_tail(x_ref, c_ref, bdw_ref, lng_ref, lnb_ref, w2_ref, b2_ref, o_ref):
    c = jnp.concatenate([c_ref[cb] for cb in range(N_COL_BLOCKS)], axis=1) + bdw_ref[...]
    mu = jnp.mean(c, axis=-1, keepdims=True)
    d = c - mu
    var = jnp.mean(d * d, axis=-1, keepdims=True)
    y = d * lax.rsqrt(var + LN_EPS) * lng_ref[...] + lnb_ref[...]
    z = y * _sigmoid(y)
    out = jnp.dot(z.astype(BF16), w2_ref[...], preferred_element_type=F32) + b2_ref[...]
    o_ref[...] = x_ref[...] + out


def _conv_prompt_kernel(x_ref, g_ref, w1_ref, b1_ref, wdw_ref, bdw_ref, lng_ref, lnb_ref,
                        w2_ref, b2_ref, o_ref, st_ref, ext_ref, c_ref, *, tm):
    i = pl.program_id(0)

    @pl.when(i == 0)
    def _():
        ext_ref[:, 0:CONV_PAD, :] = jnp.zeros((N_COL_BLOCKS, CONV_PAD, LANES), F32)

    u = _glu(x_ref, g_ref, w1_ref, b1_ref)
    for cb in range(N_COL_BLOCKS):
        ext_ref[cb, CONV_PAD:, :] = u[:, cb * LANES:(cb + 1) * LANES]
    _dwconv_rows(ext_ref, wdw_ref, c_ref, n_rows=tm, ext_row0=CONV_PAD - CONV_PREFIX,
                 out_row0=0, rc=64)
    _conv_tail(x_ref, c_ref, bdw_ref, lng_ref, lnb_ref, w2_ref, b2_ref, o_ref)
    for cb in range(N_COL_BLOCKS):
        tail = ext_ref[cb, tm:tm + CONV_PAD, :]
        ext_ref[cb, 0:CONV_PAD, :] = tail
        st_ref[:, cb * LANES:(cb + 1) * LANES] = tail


def _conv_weight_specs(const, layer):
    return [
        pl.BlockSpec((1, D_MODEL), const),
        _resident((D_MODEL, 2 * D_MODEL), layer),
        pl.BlockSpec((1, 2 * D_MODEL), const),
        pl.BlockSpec((N_COL_BLOCKS, CONV_PAD, LANES), lambda i: (0, 0, 0)),
        pl.BlockSpec((1, D_MODEL), const),
        pl.BlockSpec((1, D_MODEL), const),
        pl.BlockSpec((1, D_MODEL), const),
        _resident((D_MODEL, D_MODEL), layer),
        pl.BlockSpec((1, D_MODEL), const),
    ]


def _conv_prompt(x, weights, *, layer, tm):
    t = x.shape[0]
    tm = min(tm, t)
    assert tm >= CONV_PAD
    row = lambda i: (i, 0)
    const = lambda i: (0, 0)
    return pl.pallas_call(
        functools.partial(_conv_prompt_kernel, tm=tm),
        grid=(t // tm,),
        in_specs=[pl.BlockSpec((tm, D_MODEL), row)] + _conv_weight_specs(const, layer),
        out_specs=[pl.BlockSpec((tm, D_MODEL), row), pl.BlockSpec((CONV_PAD, D_MODEL), const)],
        out_shape=[jax.ShapeDtypeStruct((t, D_MODEL), F32),
                   jax.ShapeDtypeStruct((CONV_PAD, D_MODEL), F32)],
        scratch_shapes=[pltpu.VMEM((N_COL_BLOCKS, tm + CONV_PAD, LANES), F32),
                        pltpu.VMEM((N_COL_BLOCKS, tm, LANES), F32)],
        compiler_params=_params(1),
        name="conv_prompt",
    )(x, *weights)


def _conv_sample_kernel(x_ref, st_in_ref, g_ref, w1_ref, b1_ref, wdw_ref, bdw_ref, lng_ref,
                        lnb_ref, w2_ref, b2_ref, o_ref, st_ref, ext_ref, c_ref, *, nb, ts):
    u = _glu(x_ref, g_ref, w1_ref, b1_ref)
    seg = CONV_PAD + ts
    for b in range(nb):
        for cb in range(N_COL_BLOCKS):
            cols = slice(cb * LANES, (cb + 1) * LANES)
            ext_ref[cb, b * seg:b * seg + CONV_PAD, :] = st_in_ref[b, :, cols]
            ext_ref[cb, b * seg + CONV_PAD:(b + 1) * seg, :] = u[b * ts:(b + 1) * ts, cols]
    for b in range(nb):
        _dwconv_rows(ext_ref, wdw_ref, c_ref, n_rows=ts,
                     ext_row0=b * seg + CONV_PAD - CONV_PREFIX, out_row0=b * ts, rc=ts)
    _conv_tail(x_ref, c_ref, bdw_ref, lng_ref, lnb_ref, w2_ref, b2_ref, o_ref)
    for b in range(nb):
        for cb in range(N_COL_BLOCKS):
            st_ref[b, :, cb * LANES:(cb + 1) * LANES] = ext_ref[cb, b * seg + ts:(b + 1) * seg, :]


def _conv_sample(x, state, weights, *, layer, nb, ts):
    t = x.shape[0]
    assert ts % 8 == 0
    const = lambda i: (0, 0)
    const3 = lambda i: (0, 0, 0)
    return pl.pallas_call(
        functools.partial(_conv_sample_kernel, nb=nb, ts=ts),
        grid=(1,),
        in_specs=[pl.BlockSpec((t, D_MODEL), const),
                  pl.BlockSpec((nb, CONV_PAD, D_MODEL), const3)] + _conv_weight_specs(const, layer),
        out_specs=[pl.BlockSpec((t, D_MODEL), const),
                   pl.BlockSpec((nb, CONV_PAD, D_MODEL), const3)],
        out_shape=[jax.ShapeDtypeStruct((t, D_MODEL), F32),
                   jax.ShapeDtypeStruct((nb, CONV_PAD, D_MODEL), F32)],
        scratch_shapes=[pltpu.VMEM((N_COL_BLOCKS, nb * (CONV_PAD + ts), LANES), F32),
                        pltpu.VMEM((N_COL_BLOCKS, t, LANES), F32)],
        compiler_params=_params(1),
        name="conv_sample",
    )(x, state, *weights)


def kernel(x_prompt, x_sample, cache_k, cache_v, state_pool, state_conv, norm_mix, norm_ffn, norm_final, a_w_qkv, a_b_qkv, a_sinks, a_w_o, a_b_o, b_w_group, b_scale, c_w_pw1, c_b_pw1, c_w_dw, c_b_dw, c_ln_g, c_ln_b, c_w_pw2, c_b_pw2, f_w_gate_up, f_w_down):
    bp, seq, _ = x_prompt.shape
    nb, ts, _ = x_sample.shape
    assert bp == 1 and seq % CHUNK == 0
    wc = cache_k.shape[2]
    xp = x_prompt.reshape(seq, D_MODEL)
    xs = x_sample.reshape(nb * ts, D_MODEL)
    row = lambda a: a.reshape(1, -1)

    tables_p = _rope_tables(jnp.arange(seq))
    tables_s = tuple(jnp.tile(a, (nb, 1)) for a in _rope_tables(PAST_LEN + jnp.arange(ts)))
    g_final = row(norm_final)

    w_qkv, w_o = a_w_qkv.astype(BF16), a_w_o.astype(BF16)
    w_g = b_w_group.astype(BF16)
    w_pw1, w_pw2 = c_w_pw1.astype(BF16), c_w_pw2.astype(BF16)
    w_gu = w_d = None

    kp_l, vp_l, ks_l, vs_l = [], [], [], []
    pp_l, ps_l, cp_l, cs_l = [], [], [], []
    for i in range(DEPTH):
        kind, j = i % N_MIXERS, i // N_MIXERS
        g_mix = row(norm_mix[i])
        if kind == 0:
            b_qkv, b_o = row(a_b_qkv[j]), row(a_b_o[j])
            if i == 0:
                q, kv, w_gu, w_d = _qkv(xp, g_mix, w_qkv, b_qkv, tables_p, layer=j, tm=512,
                                        ffn_f32=(f_w_gate_up, f_w_down, 0))
            else:
                q, kv = _qkv(xp, g_mix, w_qkv, b_qkv, tables_p, layer=j, tm=512)
            xp = _attn_prompt(xp, q, kv, a_sinks[j], w_o, b_o, layer=j, tq=256)
            keep = min(WINDOW, seq)
            kp_l.append(kv[seq - keep:, :KV_DIM].reshape(1, keep, N_KV_HEADS, HEAD_DIM))
            vp_l.append(kv[seq - keep:, KV_DIM:].reshape(1, keep, N_KV_HEADS, HEAD_DIM))
            q, kv = _qkv(xs, g_mix, w_qkv, b_qkv, tables_s, layer=j, tm=nb * ts)
            xs, nk, nv = _attn_sample(xs, q, kv, cache_k[j].reshape(nb, wc, KV_DIM),
                                      cache_v[j].reshape(nb, wc, KV_DIM), a_sinks[j], w_o, b_o,
                                      layer=j, nb=nb, ts=ts, per=4)
            ks_l.append(nk.reshape(nb, wc, N_KV_HEADS, HEAD_DIM))
            vs_l.append(nv.reshape(nb, wc, N_KV_HEADS, HEAD_DIM))
        elif kind == 1:
            sc = row(b_scale[j])
            xp, hp = _pool(xp, xp, g_mix, w_g, sc, layer=j, tm=256, pos0=0, rows_per_seq=seq,
                           raw_prefix=True)
            pp_l.append(hp[-POOL_PREFIX:].reshape(1, POOL_PREFIX, D_MODEL))
            assert ts == POOL_PAD
            pre = jnp.pad(state_pool[j], ((0, 0), (POOL_PAD - POOL_PREFIX, 0), (0, 0)))
            xs, hs = _pool(xs, pre.reshape(nb * POOL_PAD, D_MODEL), g_mix, w_g, sc, layer=j,
                           tm=ts, pos0=PAST_LEN, rows_per_seq=ts, raw_prefix=False)
            ext = jnp.concatenate([state_pool[j], hs.reshape(nb, ts, D_MODEL)], axis=1)
            ps_l.append(ext[:, -POOL_PREFIX:])
        else:
            w_dw = jnp.pad(c_w_dw[j], ((0, CONV_PAD - CONV_WIDTH), (0, 0)))
            w_dw = w_dw.reshape(CONV_PAD, N_COL_BLOCKS, LANES).transpose(1, 0, 2)
            weights = (g_mix, w_pw1, row(c_b_pw1[j]), w_dw, row(c_b_dw[j]),
                       row(c_ln_g[j]), row(c_ln_b[j]), w_pw2, row(c_b_pw2[j]))
            xp, st = _conv_prompt(xp, weights, layer=j, tm=256)
            cp_l.append(st[-CONV_PREFIX:].reshape(1, CONV_PREFIX, D_MODEL))
            pre = jnp.pad(state_conv[j], ((0, 0), (CONV_PAD - CONV_PREFIX, 0), (0, 0)))
            xs, st = _conv_sample(xs, pre, weights, layer=j, nb=nb, ts=ts)
            cs_l.append(st[:, -CONV_PREFIX:])
        last = i == DEPTH - 1
        g_ffn = row(norm_ffn[i])
        xs = _ffn(xs, g_ffn, w_gu, w_d, g_final, final_norm=last, tm=256)[0]
        if last:
            xp = _ffn(xp, g_ffn, w_gu, w_d, g_final, final_norm=True, tm=512)[0]
        else:
            xp, w_gu, w_d = _ffn(xp, g_ffn, w_gu, w_d, g_final, final_norm=False, tm=512,
                                 next_f32=(f_w_gate_up, f_w_down, i + 1))
    return (xp.reshape(1, seq, D_MODEL), xs.reshape(nb, ts, D_MODEL),
            jnp.stack(kp_l), jnp.stack(vp_l), jnp.stack(ks_l), jnp.stack(vs_l),
            jnp.stack(pp_l), jnp.stack(ps_l), jnp.stack(cp_l), jnp.stack(cs_l))
```

```python
import functools

import jax
import jax.numpy as jnp
from jax import lax
from jax.experimental import pallas as pl
from jax.experimental.pallas import tpu as pltpu

F32 = jnp.float32
BF16 = jnp.bfloat16

D_MODEL = 2048
DEPTH = 4
PAST_LEN = 1024
CHUNK = 64
N_MIXERS = 3

N_HEADS = 32
N_KV_HEADS = 4
HEAD_DIM = 64
GROUP = N_HEADS // N_KV_HEADS
Q_DIM = N_HEADS * HEAD_DIM
KV_DIM = N_KV_HEADS * HEAD_DIM
QKV_DIM = Q_DIM + 2 * KV_DIM
WINDOW = 128
ROT_DIM = HEAD_DIM // 4
ROPE_THETA = 500000.0

POOL_WINDOWS = (2, 4, 8, 16)
POOL_GROUP_DIM = D_MODEL // len(POOL_WINDOWS)
POOL_PREFIX = max(POOL_WINDOWS) - 1
POOL_PAD = 16

CONV_WIDTH = 31
CONV_PREFIX = CONV_WIDTH - 1
CONV_PAD = 32

D_FF = 5632
RMS_EPS = 1e-5
LN_EPS = 1e-5

LANES = 128
N_COL_BLOCKS = D_MODEL // LANES
VMEM_LIMIT = 56 * 1024 * 1024


def _params(n_axes):
    return pltpu.CompilerParams(dimension_semantics=("arbitrary",) * n_axes,
                                vmem_limit_bytes=VMEM_LIMIT)


def _resident(shape, layer):
    return pl.BlockSpec((None,) + shape, lambda *_: (layer,) + (0,) * len(shape),
                        pipeline_mode=pl.Buffered(1))


def _rms(x, g):
    return x * lax.rsqrt(jnp.mean(x * x, axis=-1, keepdims=True) + RMS_EPS) * g


def _sigmoid(x):
    return 1.0 / (1.0 + jnp.exp(-x))


DOWN_VIEW_COLS = 1024


def _cast_specs(n, layer, step_map):
    gu_rows = D_MODEL // n
    d_rows = D_FF * D_MODEL // DOWN_VIEW_COLS // n
    assert gu_rows * n == D_MODEL and gu_rows % 16 == 0
    assert d_rows * n * DOWN_VIEW_COLS == D_FF * D_MODEL and d_rows % 16 == 0
    in_specs = [pl.BlockSpec((None, gu_rows, 2 * D_FF), lambda *ij: (layer, step_map(*ij), 0)),
                pl.BlockSpec((None, d_rows, DOWN_VIEW_COLS),
                             lambda *ij: (layer, step_map(*ij), 0))]
    out_specs = [pl.BlockSpec((gu_rows, 2 * D_FF), lambda *ij: (step_map(*ij), 0)),
                 pl.BlockSpec((d_rows, DOWN_VIEW_COLS), lambda *ij: (step_map(*ij), 0))]
    out_shape = [jax.ShapeDtypeStruct((D_MODEL, 2 * D_FF), BF16),
                 jax.ShapeDtypeStruct((D_FF * D_MODEL // DOWN_VIEW_COLS, DOWN_VIEW_COLS), BF16)]
    return in_specs, out_specs, out_shape


def _cast_args(f_w_gate_up, f_w_down):
    return [f_w_gate_up, f_w_down.reshape(f_w_down.shape[0], -1, DOWN_VIEW_COLS)]


def _cast_slab(gu32_ref, d32_ref, gu16_ref, d16_ref):
    gu16_ref[...] = gu32_ref[...].astype(BF16)
    d16_ref[...] = d32_ref[...].astype(BF16)


def _ffn_kernel(*refs, final_norm, convert_next):
    x_ref, g_ref, wg_ref, wu_ref, wd_ref, gf_ref = refs[:6]
    if convert_next:
        gu32_ref, d32_ref, o_ref, gu16_ref, d16_ref, h_ref = refs[6:]
        gu16_ref[...] = gu32_ref[...].astype(BF16)
        d16_ref[...] = d32_ref[...].astype(BF16)
    else:
        o_ref, h_ref = refs[6:]
    j = pl.program_id(1)

    @pl.when(j == 0)
    def _():
        x = x_ref[...]
        h_ref[...] = _rms(x, g_ref[...]).astype(BF16)
        o_ref[...] = x

    h = h_ref[...]
    a_g = jnp.dot(h, wg_ref[...], preferred_element_type=F32)
    a_u = jnp.dot(h, wu_ref[...], preferred_element_type=F32)
    act = (a_g * _sigmoid(a_g)) * a_u
    o_ref[...] += jnp.dot(act.astype(BF16), wd_ref[...], preferred_element_type=F32)

    if final_norm:
        @pl.when(j == pl.num_programs(1) - 1)
        def _():
            o_ref[...] = _rms(o_ref[...], gf_ref[...])


def _ffn(x, g, w_gate_up, w_down, g_final, *, final_norm, tm, tf=512, next_f32=None):
    t = x.shape[0]
    tm = min(tm, t)
    ni, nj = t // tm, D_FF // tf
    in_specs = [
        pl.BlockSpec((tm, D_MODEL), lambda i, j: (i, 0)),
        pl.BlockSpec((1, D_MODEL), lambda i, j: (0, 0)),
        pl.BlockSpec((D_MODEL, tf), lambda i, j: (0, j)),
        pl.BlockSpec((D_MODEL, tf), lambda i, j: (0, j + nj)),
        pl.BlockSpec((tf, D_MODEL), lambda i, j: (j, 0)),
        pl.BlockSpec((1, D_MODEL), lambda i, j: (0, 0)),
    ]
    out_specs = [pl.BlockSpec((tm, D_MODEL), lambda i, j: (i, 0))]
    out_shape = [jax.ShapeDtypeStruct((t, D_MODEL), F32)]
    args = [x, g, w_gate_up, w_gate_up, w_down, g_final]
    if next_f32 is not None:
        gu32, d32, layer = next_f32
        gu_rows, gu_cols = D_MODEL // ni, 2 * D_FF // nj
        d_rows = D_FF // (ni * nj)
        assert gu_rows * ni == D_MODEL and gu_rows % 16 == 0 and gu_cols % LANES == 0
        assert d_rows * ni * nj == D_FF and d_rows % 16 == 0
        in_specs += [pl.BlockSpec((None, gu_rows, gu_cols), lambda i, j: (layer, i, j)),
                     pl.BlockSpec((None, d_rows, D_MODEL), lambda i, j: (layer, i * nj + j, 0))]
        out_specs += [pl.BlockSpec((gu_rows, gu_cols), lambda i, j: (i, j)),
                      pl.BlockSpec((d_rows, D_MODEL), lambda i, j: (i * nj + j, 0))]
        out_shape += [jax.ShapeDtypeStruct((D_MODEL, 2 * D_FF), BF16),
                      jax.ShapeDtypeStruct((D_FF, D_MODEL), BF16)]
        args += [gu32, d32]
    return pl.pallas_call(
        functools.partial(_ffn_kernel, final_norm=final_norm, convert_next=next_f32 is not None),
        grid=(ni, nj),
        in_specs=in_specs,
        out_specs=out_specs,
        out_shape=out_shape,
        scratch_shapes=[pltpu.VMEM((tm, D_MODEL), BF16)],
        compiler_params=_params(2),
        name="swiglu",
    )(*args)


def _rope_tables(pos):
    inv_freq = ROPE_THETA ** (-jnp.arange(0, ROT_DIM, 2, dtype=F32) / ROT_DIM)
    ang = pos.astype(F32)[:, None] * inv_freq[None, :]
    cos, sin = jnp.cos(ang), jnp.sin(ang)
    half = ROT_DIM // 2
    rest = HEAD_DIM - ROT_DIM
    n = pos.shape[0]
    ones, zeros = jnp.ones((n, rest), F32), jnp.zeros((n, rest), F32)
    zh = jnp.zeros((n, half), F32)
    ca = jnp.concatenate([cos, cos, ones], axis=1)
    sb = jnp.concatenate([-sin, zh, zeros], axis=1)
    sc = jnp.concatenate([zh, sin, zeros], axis=1)
    rep = LANES // HEAD_DIM
    return tuple(jnp.tile(a, (1, rep)) for a in (ca, sb, sc))


def _qkv_kernel(*refs, convert):
    x_ref, g_ref, w_ref, b_ref, ca_ref, sb_ref, sc_ref = refs[:7]
    if convert:
        gu32_ref, d32_ref, q_ref, kv_ref, gu16_ref, d16_ref = refs[7:]
        _cast_slab(gu32_ref, d32_ref, gu16_ref, d16_ref)
    else:
        q_ref, kv_ref = refs[7:]
    h = _rms(x_ref[...], g_ref[...]).astype(BF16)
    qkv = jnp.dot(h, w_ref[...], preferred_element_type=F32) + b_ref[...]
    ca, sb, sc = ca_ref[...], sb_ref[...], sc_ref[...]
    half = ROT_DIM // 2
    for s in range((Q_DIM + KV_DIM) // LANES):
        blk = qkv[:, s * LANES:(s + 1) * LANES]
        r = (blk * ca + pltpu.roll(blk, LANES - half, 1) * sb + pltpu.roll(blk, half, 1) * sc)
        if s < Q_DIM // LANES:
            q_ref[:, s * LANES:(s + 1) * LANES] = (r * (HEAD_DIM ** -0.5)).astype(BF16)
        else:
            c0 = s * LANES - Q_DIM
            kv_ref[:, c0:c0 + LANES] = r
    kv_ref[:, KV_DIM:] = qkv[:, Q_DIM + KV_DIM:]


def _qkv(x, g, w, b, tables, *, layer, tm, ffn_f32=None):
    t = x.shape[0]
    tm = min(tm, t)
    n = t // tm
    row = lambda i: (i, 0)
    const = lambda i: (0, 0)
    in_specs = [
        pl.BlockSpec((tm, D_MODEL), row),
        pl.BlockSpec((1, D_MODEL), const),
        _resident((D_MODEL, QKV_DIM), layer),
        pl.BlockSpec((1, QKV_DIM), const),
        pl.BlockSpec((tm, LANES), row),
        pl.BlockSpec((tm, LANES), row),
        pl.BlockSpec((tm, LANES), row),
    ]
    out_specs = [pl.BlockSpec((tm, Q_DIM), row), pl.BlockSpec((tm, 2 * KV_DIM), row)]
    out_shape = [jax.ShapeDtypeStruct((t, Q_DIM), BF16),
                 jax.ShapeDtypeStruct((t, 2 * KV_DIM), F32)]
    args = [x, g, w, b, *tables]
    if ffn_f32 is not None:
        gu32, d32, ffn_layer = ffn_f32
        cast_in, cast_out, cast_shape = _cast_specs(n, ffn_layer, lambda i: i)
        in_specs += cast_in
        out_specs += cast_out
        out_shape += cast_shape
        args += _cast_args(gu32, d32)
    return pl.pallas_call(
        functools.partial(_qkv_kernel, convert=ffn_f32 is not None),
        grid=(n,),
        in_specs=in_specs,
        out_specs=out_specs,
        out_shape=out_shape,
        compiler_params=_params(1),
        name="qkv_rope",
    )(*args)


def _half_masked(blk):
    lo = lax.broadcasted_iota(jnp.int32, blk.shape, 1) < HEAD_DIM
    swapped = pltpu.roll(blk, HEAD_DIM, 1)
    zero = jnp.zeros_like(blk)
    sides = ((jnp.where(lo, blk, zero), jnp.where(lo, zero, swapped)),
             (jnp.where(lo, swapped, zero), jnp.where(lo, zero, blk)))
    return [tuple(x.astype(BF16) for x in side) for side in sides]


def _softmax_block(s, sink, valid):
    if valid is not None:
        s = jnp.where(valid, s, -jnp.inf)
    m = jnp.maximum(jnp.max(s, axis=-1, keepdims=True), sink)
    p = jnp.exp(s - m)
    denom = jnp.sum(p, axis=-1, keepdims=True) + jnp.exp(sink - m)
    return p.astype(BF16), 1.0 / denom


def _attention(q_ref, q_row0, r, k_parts, v_parts, sinks_ref, valid, store):
    pairs = GROUP // 2
    nt = (((1,), (1,)), ((), ()))

    def gather(parts, m):
        return jnp.concatenate([ref[r0:r0 + n, off + m * LANES:off + (m + 1) * LANES]
                                for ref, r0, n, off in parts], axis=0)

    for m in range(KV_DIM // LANES):
        k_ops = _half_masked(gather(k_parts, m))
        v_ops = _half_masked(gather(v_parts, m))
        for gi in range(LANES // HEAD_DIM):
            p0 = (m * (LANES // HEAD_DIM) + gi) * pairs
            qg = jnp.concatenate(
                [q_ref[q_row0:q_row0 + r, (p0 + pi) * LANES:(p0 + pi + 1) * LANES]
                 for pi in range(pairs)], axis=0)
            pv, rcp = [], []
            for par in range(2):
                s = lax.dot_general(qg, k_ops[gi][par], nt, preferred_element_type=F32)
                blocks = [_softmax_block(s[pi * r:(pi + 1) * r], sinks_ref[2 * (p0 + pi) + par],
                                         valid) for pi in range(pairs)]
                p_all = jnp.concatenate([b[0] for b in blocks], axis=0)
                pv.append(jnp.dot(p_all, v_ops[gi][par], preferred_element_type=F32))
                rcp.append([b[1] for b in blocks])
            for pi in range(pairs):
                rows = slice(pi * r, (pi + 1) * r)
                o = pv[0][rows] * rcp[0][pi] + pv[1][rows] * rcp[1][pi]
                store(p0 + pi, o.astype(BF16))


def _attn_prompt_kernel(*refs, tq, convert):
    sinks_ref, x_ref, q_ref, kvp_ref, kvc_ref, wo_ref, bo_ref = refs[:7]
    if convert:
        gu32_ref, d32_ref, o_ref, gu16_ref, d16_ref, o_scr = refs[7:]
        _cast_slab(gu32_ref, d32_ref, gu16_ref, d16_ref)
    else:
        o_ref, o_scr = refs[7:]
    i = pl.program_id(0)
    sub = WINDOW
    n_prev = sub // CHUNK
    cq = lax.broadcasted_iota(jnp.int32, (sub, 2 * sub), 0) // CHUNK
    ck = lax.broadcasted_iota(jnp.int32, (sub, 2 * sub), 1) // CHUNK - n_prev
    band = (ck <= cq) & (ck >= cq - WINDOW // CHUNK)
    for t0 in range(0, tq, sub):
        if t0 == 0:
            before = kvp_ref, 0
            valid = band & ((ck >= 0) | (i > 0))
        else:
            before = kvc_ref, t0 - sub
            valid = band

        def store(pair, val, t0=t0):
            o_scr[t0:t0 + sub, pair * LANES:(pair + 1) * LANES] = val

        k_parts = [(before[0], before[1], sub, 0), (kvc_ref, t0, sub, 0)]
        v_parts = [(before[0], before[1], sub, KV_DIM), (kvc_ref, t0, sub, KV_DIM)]
        _attention(q_ref, t0, sub, k_parts, v_parts, sinks_ref, valid, store)
    y = jnp.dot(o_scr[...], wo_ref[...], preferred_element_type=F32) + bo_ref[...]
    o_ref[...] = x_ref[...] + y


def _attn_prompt(x, q, kv, sinks, wo, bo, *, layer, tq, ffn_f32=None):
    t = x.shape[0]
    tq = min(tq, t)
    assert tq % WINDOW == 0 and t % tq == 0
    per = tq // WINDOW
    n = t // tq
    row = lambda i: (i, 0)
    const = lambda i: (0, 0)
    in_specs = [
        pl.BlockSpec(memory_space=pltpu.SMEM),
        pl.BlockSpec((tq, D_MODEL), row),
        pl.BlockSpec((tq, Q_DIM), row),
        pl.BlockSpec((WINDOW, 2 * KV_DIM), lambda i: (jnp.maximum(i * per - 1, 0), 0)),
        pl.BlockSpec((tq, 2 * KV_DIM), row),
        _resident((Q_DIM, D_MODEL), layer),
        pl.BlockSpec((1, D_MODEL), const),
    ]
    out_specs = [pl.BlockSpec((tq, D_MODEL), row)]
    out_shape = [jax.ShapeDtypeStruct((t, D_MODEL), F32)]
    args = [sinks, x, q, kv, kv, wo, bo]
    if ffn_f32 is not None:
        gu32, d32, ffn_layer = ffn_f32
        cast_in, cast_out, cast_shape = _cast_specs(n, ffn_layer, lambda i: i)
        in_specs += cast_in
        out_specs += cast_out
        out_shape += cast_shape
        args += _cast_args(gu32, d32)
    return pl.pallas_call(
        functools.partial(_attn_prompt_kernel, tq=tq, convert=ffn_f32 is not None),
        grid=(n,),
        in_specs=in_specs,
        out_specs=out_specs,
        out_shape=out_shape,
        scratch_shapes=[pltpu.VMEM((tq, Q_DIM), BF16)],
        compiler_params=_params(1),
        name="attn_prompt",
    )(*args)


def _attn_sample_kernel(sinks_ref, x_ref, q_ref, kv_ref, ck_ref, cv_ref, wo_ref, bo_ref,
                        o_ref, nk_ref, nv_ref, o_scr, *, ts, wc, per):
    step = pl.program_id(0)
    for bb in range(per):
        r0 = bb * ts
        nk_ref[bb, 0:wc - ts, :] = ck_ref[bb, ts:, :]
        nk_ref[bb, wc - ts:, :] = kv_ref[r0:r0 + ts, :KV_DIM]
        nv_ref[bb, 0:wc - ts, :] = cv_ref[bb, ts:, :]
        nv_ref[bb, wc - ts:, :] = kv_ref[r0:r0 + ts, KV_DIM:]
        row0 = pl.multiple_of((step * per + bb) * ts, ts)

        def store(pair, val, row0=row0):
            o_scr[pl.ds(row0, ts), pair * LANES:(pair + 1) * LANES] = val

        k_parts = [(ck_ref.at[bb], 0, wc, 0), (kv_ref, r0, ts, 0)]
        v_parts = [(cv_ref.at[bb], 0, wc, 0), (kv_ref, r0, ts, KV_DIM)]
        _attention(q_ref, r0, ts, k_parts, v_parts, sinks_ref, None, store)

    @pl.when(step == pl.num_programs(0) - 1)
    def _():
        y = jnp.dot(o_scr[...], wo_ref[...], preferred_element_type=F32) + bo_ref[...]
        o_ref[...] = x_ref[...] + y


def _attn_sample(x, q, kv, cache_k, cache_v, sinks, wo, bo, *, layer, nb, ts, per):
    t = x.shape[0]
    wc = cache_k.shape[1]
    assert ts <= wc and ts % 16 == 0 and nb % per == 0
    row = lambda b: (b, 0)
    const = lambda b: (0, 0)
    batch = lambda b: (b, 0, 0)
    return pl.pallas_call(
        functools.partial(_attn_sample_kernel, ts=ts, wc=wc, per=per),
        grid=(nb // per,),
        in_specs=[
            pl.BlockSpec(memory_space=pltpu.SMEM),
            pl.BlockSpec((t, D_MODEL), const),
            pl.BlockSpec((per * ts, Q_DIM), row),
            pl.BlockSpec((per * ts, 2 * KV_DIM), row),
            pl.BlockSpec((per, wc, KV_DIM), batch),
            pl.BlockSpec((per, wc, KV_DIM), batch),
            _resident((Q_DIM, D_MODEL), layer),
            pl.BlockSpec((1, D_MODEL), const),
        ],
        out_specs=[pl.BlockSpec((t, D_MODEL), const),
                   pl.BlockSpec((per, wc, KV_DIM), batch),
                   pl.BlockSpec((per, wc, KV_DIM), batch)],
        out_shape=[jax.ShapeDtypeStruct((t, D_MODEL), F32),
                   jax.ShapeDtypeStruct((nb, wc, KV_DIM), F32),
                   jax.ShapeDtypeStruct((nb, wc, KV_DIM), F32)],
        scratch_shapes=[pltpu.VMEM((t, Q_DIM), BF16)],
        compiler_params=_params(1),
        name="attn_sample",
    )(sinks, x, q, kv, cache_k, cache_v, wo, bo)


def _pool_kernel(x_ref, pre_ref, g_ref, w_ref, sc_ref, o_ref, h_out_ref, ext_ref,
                 *, tm, pos0, rows_per_seq, raw_prefix):
    i = pl.program_id(0)
    x = x_ref[...]
    g = g_ref[...]
    h = _rms(x, g)
    if raw_prefix:
        pre = jnp.where(i > 0, _rms(pre_ref[...], g), 0.0)
    else:
        pre = pre_ref[...]
    ext_ref[0:POOL_PAD, :] = pre
    ext_ref[POOL_PAD:, :] = h
    h_out_ref[...] = h[tm - POOL_PAD:, :]
    row = lax.broadcasted_iota(jnp.int32, (tm, 1), 0)
    pos = pos0 + (i * tm + row) % rows_per_seq
    for gi, w in enumerate(POOL_WINDOWS):
        c0 = gi * POOL_GROUP_DIM
        sl = slice(c0, c0 + POOL_GROUP_DIM)
        acc = h[:, sl]
        for j in range(1, w):
            acc = acc + ext_ref[POOL_PAD - j:POOL_PAD - j + tm, sl]
        cnt = jnp.minimum(pos + 1, w).astype(F32)
        pooled = acc / cnt - h[:, sl]
        mixed = jnp.dot(pooled.astype(BF16), w_ref[gi], preferred_element_type=F32)
        o_ref[:, sl] = x[:, sl] + mixed * sc_ref[:, sl]


def _pool(x, prefix, g, w_group, scale, *, layer, tm, pos0, rows_per_seq, raw_prefix):
    t = x.shape[0]
    tm = min(tm, t)
    nblk = tm // POOL_PAD
    if raw_prefix:
        pre_map = lambda i: (jnp.maximum(i * nblk - 1, 0), 0)
    else:
        pre_map = lambda i: (i, 0)
    row = lambda i: (i, 0)
    const = lambda i: (0, 0)
    return pl.pallas_call(
        functools.partial(_pool_kernel, tm=tm, pos0=pos0, rows_per_seq=rows_per_seq,
                          raw_prefix=raw_prefix),
        grid=(t // tm,),
        in_specs=[
            pl.BlockSpec((tm, D_MODEL), row),
            pl.BlockSpec((POOL_PAD, D_MODEL), pre_map),
            pl.BlockSpec((1, D_MODEL), const),
            _resident((len(POOL_WINDOWS), POOL_GROUP_DIM, POOL_GROUP_DIM), layer),
            pl.BlockSpec((1, D_MODEL), const),
        ],
        out_specs=[pl.BlockSpec((tm, D_MODEL), row), pl.BlockSpec((POOL_PAD, D_MODEL), row)],
        out_shape=[jax.ShapeDtypeStruct((t, D_MODEL), F32),
                   jax.ShapeDtypeStruct((t // tm * POOL_PAD, D_MODEL), F32)],
        scratch_shapes=[pltpu.VMEM((tm + POOL_PAD, D_MODEL), F32)],
        compiler_params=_params(1),
        name="pool_mix",
    )(x, prefix, g, w_group, scale)


def _glu(x_ref, g_ref, w1_ref, b1_ref):
    h = _rms(x_ref[...], g_ref[...]).astype(BF16)
    a = jnp.dot(h, w1_ref[...], preferred_element_type=F32) + b1_ref[...]
    return a[:, :D_MODEL] * _sigmoid(a[:, D_MODEL:])


def _dwconv_rows(ext_ref, wdw_ref, c_ref, *, n_rows, ext_row0, out_row0, rc):
    def col_block(cb, carry):
        for r0 in range(0, n_rows, rc):
            acc = jnp.zeros((rc, LANES), F32)
            for j in range(CONV_WIDTH):
                start = ext_row0 + r0 + j
                acc = acc + ext_ref[cb, start:start + rc, :] * wdw_ref[cb, j:j + 1, :]
            c_ref[cb, out_row0 + r0:out_row0 + r0 + rc, :] = acc
        return carry

    lax.fori_loop(0, N_COL_BLOCKS, col_block, 0)


def _conv_tail(x_ref, c_ref, bdw_ref, lng_ref, lnb_ref, w2_ref, b2_ref, o_ref):
    c = jnp.concatenate([c_ref[cb] for cb in range(N_COL_BLOCKS)], axis=1) + bdw_ref[...]
    mu = jnp.mean(c, axis=-1, keepdims=True)
    d = c - mu
    var = jnp.mean(d * d, axis=-1, keepdims=True)
    y = d * lax.rsqrt(var + LN_EPS) * lng_ref[...] + lnb_ref[...]
    z = y * _sigmoid(y)
    out = jnp.dot(z.astype(BF16), w2_ref[...], preferred_element_type=F32) + b2_ref[...]
    o_ref[...] = x_ref[...] + out


def _conv_prompt_kernel(x_ref, g_ref, w1_ref, b1_ref, wdw_ref, bdw_ref, lng_ref, lnb_ref,
                        w2_ref, b2_ref, o_ref, st_ref, ext_ref, c_ref, *, tm):
    i = pl.program_id(0)

    @pl.when(i == 0)
    def _():
        ext_ref[:, 0:CONV_PAD, :] = jnp.zeros((N_COL_BLOCKS, CONV_PAD, LANES), F32)

    u = _glu(x_ref, g_ref, w1_ref, b1_ref)
    for cb in range(N_COL_BLOCKS):
        ext_ref[cb, CONV_PAD:, :] = u[:, cb * LANES:(cb + 1) * LANES]
    _dwconv_rows(ext_ref, wdw_ref, c_ref, n_rows=tm, ext_row0=CONV_PAD - CONV_PREFIX,
                 out_row0=0, rc=64)
    _conv_tail(x_ref, c_ref, bdw_ref, lng_ref, lnb_ref, w2_ref, b2_ref, o_ref)
    for cb in range(N_COL_BLOCKS):
        tail = ext_ref[cb, tm:tm + CONV_PAD, :]
        ext_ref[cb, 0:CONV_PAD, :] = tail
        st_ref[:, cb * LANES:(cb + 1) * LANES] = tail


def _conv_weight_specs(const, layer):
    return [
        pl.BlockSpec((1, D_MODEL), const),
        _resident((D_MODEL, 2 * D_MODEL), layer),
        pl.BlockSpec((1, 2 * D_MODEL), const),
        pl.BlockSpec((N_COL_BLOCKS, CONV_PAD, LANES), lambda i: (0, 0, 0)),
        pl.BlockSpec((1, D_MODEL), const),
        pl.BlockSpec((1, D_MODEL), const),
        pl.BlockSpec((1, D_MODEL), const),
        _resident((D_MODEL, D_MODEL), layer),
        pl.BlockSpec((1, D_MODEL), const),
    ]


def _conv_prompt(x, weights, *, layer, tm):
    t = x.shape[0]
    tm = min(tm, t)
    assert tm >= CONV_PAD
    row = lambda i: (i, 0)
    const = lambda i: (0, 0)
    return pl.pallas_call(
        functools.partial(_conv_prompt_kernel, tm=tm),
        grid=(t // tm,),
        in_specs=[pl.BlockSpec((tm, D_MODEL), row)] + _conv_weight_specs(const, layer),
        out_specs=[pl.BlockSpec((tm, D_MODEL), row), pl.BlockSpec((CONV_PAD, D_MODEL), const)],
        out_shape=[jax.ShapeDtypeStruct((t, D_MODEL), F32),
                   jax.ShapeDtypeStruct((CONV_PAD, D_MODEL), F32)],
        scratch_shapes=[pltpu.VMEM((N_COL_BLOCKS, tm + CONV_PAD, LANES), F32),
                        pltpu.VMEM((N_COL_BLOCKS, tm, LANES), F32)],
        compiler_params=_params(1),
        name="conv_prompt",
    )(x, *weights)


def _conv_sample_kernel(x_ref, st_in_ref, g_ref, w1_ref, b1_ref, wdw_ref, bdw_ref, lng_ref,
                        lnb_ref, w2_ref, b2_ref, o_ref, st_ref, ext_ref, c_ref, *, nb, ts):
    u = _glu(x_ref, g_ref, w1_ref, b1_ref)
    seg = CONV_PAD + ts
    for b in range(nb):
        for cb in range(N_COL_BLOCKS):
            cols = slice(cb * LANES, (cb + 1) * LANES)
            ext_ref[cb, b * seg:b * seg + CONV_PAD, :] = st_in_ref[b, :, cols]
            ext_ref[cb, b * seg + CONV_PAD:(b + 1) * seg, :] = u[b * ts:(b + 1) * ts, cols]
    for b in range(nb):
        _dwconv_rows(ext_ref, wdw_ref, c_ref, n_rows=ts,
                     ext_row0=b * seg + CONV_PAD - CONV_PREFIX, out_row0=b * ts, rc=ts)
    _conv_tail(x_ref, c_ref, bdw_ref, lng_ref, lnb_ref, w2_ref, b2_ref, o_ref)
    for b in range(nb):
        for cb in range(N_COL_BLOCKS):
            st_ref[b, :, cb * LANES:(cb + 1) * LANES] = ext_ref[cb, b * seg + ts:(b + 1) * seg, :]


def _conv_sample(x, state, weights, *, layer, nb, ts):
    t = x.shape[0]
    assert ts % 8 == 0
    const = lambda i: (0, 0)
    const3 = lambda i: (0, 0, 0)
    return pl.pallas_call(
        functools.partial(_conv_sample_kernel, nb=nb, ts=ts),
        grid=(1,),
        in_specs=[pl.BlockSpec((t, D_MODEL), const),
                  pl.BlockSpec((nb, CONV_PAD, D_MODEL), const3)] + _conv_weight_specs(const, layer),
        out_specs=[pl.BlockSpec((t, D_MODEL), const),
                   pl.BlockSpec((nb, CONV_PAD, D_MODEL), const3)],
        out_shape=[jax.ShapeDtypeStruct((t, D_MODEL), F32),
                   jax.ShapeDtypeStruct((nb, CONV_PAD, D_MODEL), F32)],
        scratch_shapes=[pltpu.VMEM((N_COL_BLOCKS, nb * (CONV_PAD + ts), LANES), F32),
                        pltpu.VMEM((N_COL_BLOCKS, t, LANES), F32)],
        compiler_params=_params(1),
        name="conv_sample",
    )(x, state, *weights)


def kernel(x_prompt, x_sample, cache_k, cache_v, state_pool, state_conv, norm_mix, norm_ffn, norm_final, a_w_qkv, a_b_qkv, a_sinks, a_w_o, a_b_o, b_w_group, b_scale, c_w_pw1, c_b_pw1, c_w_dw, c_b_dw, c_ln_g, c_ln_b, c_w_pw2, c_b_pw2, f_w_gate_up, f_w_down):
    bp, seq, _ = x_prompt.shape
    nb, ts, _ = x_sample.shape
    assert bp == 1 and seq % CHUNK == 0
    wc = cache_k.shape[2]
    xp = x_prompt.reshape(seq, D_MODEL)
    xs = x_sample.reshape(nb * ts, D_MODEL)
    row = lambda a: a.reshape(1, -1)

    tables_p = _rope_tables(jnp.arange(seq))
    tables_s = tuple(jnp.tile(a, (nb, 1)) for a in _rope_tables(PAST_LEN + jnp.arange(ts)))
    g_final = row(norm_final)

    w_qkv, w_o = a_w_qkv.astype(BF16), a_w_o.astype(BF16)
    w_g = b_w_group.astype(BF16)
    w_pw1, w_pw2 = c_w_pw1.astype(BF16), c_w_pw2.astype(BF16)
    ffn_w = {}
    stacked = (f_w_gate_up, f_w_down)

    kp_l, vp_l, ks_l, vs_l = [], [], [], []
    pp_l, ps_l, cp_l, cs_l = [], [], [], []
    for i in range(DEPTH):
        kind, j = i % N_MIXERS, i // N_MIXERS
        g_mix = row(norm_mix[i])
        if kind == 0:
            b_qkv, b_o = row(a_b_qkv[j]), row(a_b_o[j])
            q, kv, w_gu, w_d = _qkv(xp, g_mix, w_qkv, b_qkv, tables_p, layer=j, tm=512,
                                    ffn_f32=stacked + (i,))
            ffn_w[i] = (w_gu, w_d.reshape(D_FF, D_MODEL))
            if i + 1 < DEPTH and (i + 1) % N_MIXERS != 0:
                xp, w_gu, w_d = _attn_prompt(xp, q, kv, a_sinks[j], w_o, b_o, layer=j, tq=256,
                                             ffn_f32=stacked + (i + 1,))
                ffn_w[i + 1] = (w_gu, w_d.reshape(D_FF, D_MODEL))
            else:
                xp = _attn_prompt(xp, q, kv, a_sinks[j], w_o, b_o, layer=j, tq=256)[0]
            keep = min(WINDOW, seq)
            kp_l.append(kv[seq - keep:, :KV_DIM].reshape(1, keep, N_KV_HEADS, HEAD_DIM))
            vp_l.append(kv[seq - keep:, KV_DIM:].reshape(1, keep, N_KV_HEADS, HEAD_DIM))
            q, kv = _qkv(xs, g_mix, w_qkv, b_qkv, tables_s, layer=j, tm=nb * ts)
            xs, nk, nv = _attn_sample(xs, q, kv, cache_k[j].reshape(nb, wc, KV_DIM),
                                      cache_v[j].reshape(nb, wc, KV_DIM), a_sinks[j], w_o, b_o,
                                      layer=j, nb=nb, ts=ts, per=4)
            ks_l.append(nk.reshape(nb, wc, N_KV_HEADS, HEAD_DIM))
            vs_l.append(nv.reshape(nb, wc, N_KV_HEADS, HEAD_DIM))
        elif kind == 1:
            sc = row(b_scale[j])
            xp, hp = _pool(xp, xp, g_mix, w_g, sc, layer=j, tm=256, pos0=0, rows_per_seq=seq,
                           raw_prefix=True)
            pp_l.append(hp[-POOL_PREFIX:].reshape(1, POOL_PREFIX, D_MODEL))
            assert ts == POOL_PAD
            pre = jnp.pad(state_pool[j], ((0, 0), (POOL_PAD - POOL_PREFIX, 0), (0, 0)))
            xs, hs = _pool(xs, pre.reshape(nb * POOL_PAD, D_MODEL), g_mix, w_g, sc, layer=j,
                           tm=ts, pos0=PAST_LEN, rows_per_seq=ts, raw_prefix=False)
            ext = jnp.concatenate([state_pool[j], hs.reshape(nb, ts, D_MODEL)], axis=1)
            ps_l.append(ext[:, -POOL_PREFIX:])
        else:
            w_dw = jnp.pad(c_w_dw[j], ((0, CONV_PAD - CONV_WIDTH), (0, 0)))
            w_dw = w_dw.reshape(CONV_PAD, N_COL_BLOCKS, LANES).transpose(1, 0, 2)
            weights = (g_mix, w_pw1, row(c_b_pw1[j]), w_dw, row(c_b_dw[j]),
                       row(c_ln_g[j]), row(c_ln_b[j]), w_pw2, row(c_b_pw2[j]))
            xp, st = _conv_prompt(xp, weights, layer=j, tm=256)
            cp_l.append(st[-CONV_PREFIX:].reshape(1, CONV_PREFIX, D_MODEL))
            pre = jnp.pad(state_conv[j], ((0, 0), (CONV_PAD - CONV_PREFIX, 0), (0, 0)))
            xs, st = _conv_sample(xs, pre, weights, layer=j, nb=nb, ts=ts)
            cs_l.append(st[:, -CONV_PREFIX:])
        last = i == DEPTH - 1
        g_ffn = row(norm_ffn[i])
        w_gu, w_d = ffn_w[i]
        xs = _ffn(xs, g_ffn, w_gu, w_d, g_final, final_norm=last, tm=256)[0]
        if last or i + 1 in ffn_w or (i + 1) % N_MIXERS == 0:
            xp = _ffn(xp, g_ffn, w_gu, w_d, g_final, final_norm=last, tm=512)[0]
        else:
            xp, w_gu, w_d = _ffn(xp, g_ffn, w_gu, w_d, g_final, final_norm=False, tm=512,
                                 next_f32=stacked + (i + 1,))
            ffn_w[i + 1] = (w_gu, w_d)
    return (xp.reshape(1, seq, D_MODEL), xs.reshape(nb, ts, D_MODEL),
            jnp.stack(kp_l), jnp.stack(vp_l), jnp.stack(ks_l), jnp.stack(vs_l),
            jnp.stack(pp_l), jnp.stack(ps_l), jnp.stack(cp_l), jnp.stack(cs_l))
```

```python
import functools

import jax
import jax.numpy as jnp
from jax import lax
from jax.experimental import pallas as pl
from jax.experimental.pallas import tpu as pltpu

F32 = jnp.float32
BF16 = jnp.bfloat16

D_MODEL = 2048
DEPTH = 4
PAST_LEN = 1024
CHUNK = 64
N_MIXERS = 3

N_HEADS = 32
N_KV_HEADS = 4
HEAD_DIM = 64
GROUP = N_HEADS // N_KV_HEADS
Q_DIM = N_HEADS * HEAD_DIM
KV_DIM = N_KV_HEADS * HEAD_DIM
QKV_DIM = Q_DIM + 2 * KV_DIM
WINDOW = 128
ROT_DIM = HEAD_DIM // 4
ROPE_THETA = 500000.0

POOL_WINDOWS = (2, 4, 8, 16)
POOL_GROUP_DIM = D_MODEL // len(POOL_WINDOWS)
POOL_PREFIX = max(POOL_WINDOWS) - 1
POOL_PAD = 16

CONV_WIDTH = 31
CONV_PREFIX = CONV_WIDTH - 1
CONV_PAD = 32

D_FF = 5632
RMS_EPS = 1e-5
LN_EPS = 1e-5

LANES = 128
N_COL_BLOCKS = D_MODEL // LANES
VMEM_LIMIT = 56 * 1024 * 1024


def _params(n_axes):
    return pltpu.CompilerParams(dimension_semantics=("arbitrary",) * n_axes,
                                vmem_limit_bytes=VMEM_LIMIT)


def _resident(shape, layer):
    return pl.BlockSpec((None,) + shape, lambda *_: (layer,) + (0,) * len(shape),
                        pipeline_mode=pl.Buffered(1))


def _rms(x, g):
    return x * lax.rsqrt(jnp.mean(x * x, axis=-1, keepdims=True) + RMS_EPS) * g


def _sigmoid(x):
    return 1.0 / (1.0 + jnp.exp(-x))


def _cast_specs(n_gu, n_d, layer, gu_map, d_map):
    gu_rows, d_rows = D_MODEL // n_gu, D_FF // n_d
    assert gu_rows * n_gu == D_MODEL and gu_rows % 16 == 0
    assert d_rows * n_d == D_FF and d_rows % 16 == 0
    in_specs = [pl.BlockSpec((None, gu_rows, 2 * D_FF), lambda *ij: (layer, gu_map(*ij), 0)),
                pl.BlockSpec((None, d_rows, D_MODEL), lambda *ij: (layer, d_map(*ij), 0))]
    out_specs = [pl.BlockSpec((gu_rows, 2 * D_FF), lambda *ij: (gu_map(*ij), 0)),
                 pl.BlockSpec((d_rows, D_MODEL), lambda *ij: (d_map(*ij), 0))]
    out_shape = [jax.ShapeDtypeStruct((D_MODEL, 2 * D_FF), BF16),
                 jax.ShapeDtypeStruct((D_FF, D_MODEL), BF16)]
    return in_specs, out_specs, out_shape


def _cast_slab(gu32_ref, d32_ref, gu16_ref, d16_ref):
    gu16_ref[...] = gu32_ref[...].astype(BF16)
    d16_ref[...] = d32_ref[...].astype(BF16)


def _ffn_kernel(*refs, final_norm, convert_next):
    x_ref, g_ref, wg_ref, wu_ref, wd_ref, gf_ref = refs[:6]
    if convert_next:
        gu32_ref, d32_ref, o_ref, gu16_ref, d16_ref, h_ref = refs[6:]
        gu16_ref[...] = gu32_ref[...].astype(BF16)
        d16_ref[...] = d32_ref[...].astype(BF16)
    else:
        o_ref, h_ref = refs[6:]
    j = pl.program_id(1)

    @pl.when(j == 0)
    def _():
        x = x_ref[...]
        h_ref[...] = _rms(x, g_ref[...]).astype(BF16)
        o_ref[...] = x

    h = h_ref[...]
    a_g = jnp.dot(h, wg_ref[...], preferred_element_type=F32)
    a_u = jnp.dot(h, wu_ref[...], preferred_element_type=F32)
    act = (a_g * _sigmoid(a_g)) * a_u
    o_ref[...] += jnp.dot(act.astype(BF16), wd_ref[...], preferred_element_type=F32)

    if final_norm:
        @pl.when(j == pl.num_programs(1) - 1)
        def _():
            o_ref[...] = _rms(o_ref[...], gf_ref[...])


def _ffn(x, g, w_gate_up, w_down, g_final, *, final_norm, tm, tf=512, next_f32=None):
    t = x.shape[0]
    tm = min(tm, t)
    ni, nj = t // tm, D_FF // tf
    in_specs = [
        pl.BlockSpec((tm, D_MODEL), lambda i, j: (i, 0)),
        pl.BlockSpec((1, D_MODEL), lambda i, j: (0, 0)),
        pl.BlockSpec((D_MODEL, tf), lambda i, j: (0, j)),
        pl.BlockSpec((D_MODEL, tf), lambda i, j: (0, j + nj)),
        pl.BlockSpec((tf, D_MODEL), lambda i, j: (j, 0)),
        pl.BlockSpec((1, D_MODEL), lambda i, j: (0, 0)),
    ]
    out_specs = [pl.BlockSpec((tm, D_MODEL), lambda i, j: (i, 0))]
    out_shape = [jax.ShapeDtypeStruct((t, D_MODEL), F32)]
    args = [x, g, w_gate_up, w_gate_up, w_down, g_final]
    if next_f32 is not None:
        gu32, d32, layer = next_f32
        gu_rows, gu_cols = D_MODEL // ni, 2 * D_FF // nj
        d_rows = D_FF // (ni * nj)
        assert gu_rows * ni == D_MODEL and gu_rows % 16 == 0 and gu_cols % LANES == 0
        assert d_rows * ni * nj == D_FF and d_rows % 16 == 0
        in_specs += [pl.BlockSpec((None, gu_rows, gu_cols), lambda i, j: (layer, i, j)),
                     pl.BlockSpec((None, d_rows, D_MODEL), lambda i, j: (layer, i * nj + j, 0))]
        out_specs += [pl.BlockSpec((gu_rows, gu_cols), lambda i, j: (i, j)),
                      pl.BlockSpec((d_rows, D_MODEL), lambda i, j: (i * nj + j, 0))]
        out_shape += [jax.ShapeDtypeStruct((D_MODEL, 2 * D_FF), BF16),
                      jax.ShapeDtypeStruct((D_FF, D_MODEL), BF16)]
        args += [gu32, d32]
    return pl.pallas_call(
        functools.partial(_ffn_kernel, final_norm=final_norm, convert_next=next_f32 is not None),
        grid=(ni, nj),
        in_specs=in_specs,
        out_specs=out_specs,
        out_shape=out_shape,
        scratch_shapes=[pltpu.VMEM((tm, D_MODEL), BF16)],
        compiler_params=_params(2),
        name="swiglu",
    )(*args)


def _rope_tables(pos):
    inv_freq = ROPE_THETA ** (-jnp.arange(0, ROT_DIM, 2, dtype=F32) / ROT_DIM)
    ang = pos.astype(F32)[:, None] * inv_freq[None, :]
    cos, sin = jnp.cos(ang), jnp.sin(ang)
    half = ROT_DIM // 2
    rest = HEAD_DIM - ROT_DIM
    n = pos.shape[0]
    ones, zeros = jnp.ones((n, rest), F32), jnp.zeros((n, rest), F32)
    zh = jnp.zeros((n, half), F32)
    ca = jnp.concatenate([cos, cos, ones], axis=1)
    sb = jnp.concatenate([-sin, zh, zeros], axis=1)
    sc = jnp.concatenate([zh, sin, zeros], axis=1)
    rep = LANES // HEAD_DIM
    return tuple(jnp.tile(a, (1, rep)) for a in (ca, sb, sc))


def _qkv_kernel(*refs, convert):
    x_ref, g_ref, w_ref, b_ref, ca_ref, sb_ref, sc_ref = refs[:7]
    if convert:
        gu32_ref, d32_ref, q_ref, kv_ref, gu16_ref, d16_ref = refs[7:]
        _cast_slab(gu32_ref, d32_ref, gu16_ref, d16_ref)
    else:
        q_ref, kv_ref = refs[7:]
    h = _rms(x_ref[...], g_ref[...]).astype(BF16)
    qkv = jnp.dot(h, w_ref[...], preferred_element_type=F32) + b_ref[...]
    ca, sb, sc = ca_ref[...], sb_ref[...], sc_ref[...]
    half = ROT_DIM // 2
    for s in range((Q_DIM + KV_DIM) // LANES):
        blk = qkv[:, s * LANES:(s + 1) * LANES]
        r = (blk * ca + pltpu.roll(blk, LANES - half, 1) * sb + pltpu.roll(blk, half, 1) * sc)
        if s < Q_DIM // LANES:
            q_ref[:, s * LANES:(s + 1) * LANES] = (r * (HEAD_DIM ** -0.5)).astype(BF16)
        else:
            c0 = s * LANES - Q_DIM
            kv_ref[:, c0:c0 + LANES] = r
    kv_ref[:, KV_DIM:] = qkv[:, Q_DIM + KV_DIM:]


def _qkv(x, g, w, b, tables, *, layer, tm, ffn_f32=None):
    t = x.shape[0]
    tm = min(tm, t)
    n = t // tm
    row = lambda i: (i, 0)
    const = lambda i: (0, 0)
    in_specs = [
        pl.BlockSpec((tm, D_MODEL), row),
        pl.BlockSpec((1, D_MODEL), const),
        _resident((D_MODEL, QKV_DIM), layer),
        pl.BlockSpec((1, QKV_DIM), const),
        pl.BlockSpec((tm, LANES), row),
        pl.BlockSpec((tm, LANES), row),
        pl.BlockSpec((tm, LANES), row),
    ]
    out_specs = [pl.BlockSpec((tm, Q_DIM), row), pl.BlockSpec((tm, 2 * KV_DIM), row)]
    out_shape = [jax.ShapeDtypeStruct((t, Q_DIM), BF16),
                 jax.ShapeDtypeStruct((t, 2 * KV_DIM), F32)]
    args = [x, g, w, b, *tables]
    if ffn_f32 is not None:
        gu32, d32, ffn_layer = ffn_f32
        cast_in, cast_out, cast_shape = _cast_specs(n, n, ffn_layer, lambda i: i, lambda i: i)
        in_specs += cast_in
        out_specs += cast_out
        out_shape += cast_shape
        args += [gu32, d32]
    return pl.pallas_call(
        functools.partial(_qkv_kernel, convert=ffn_f32 is not None),
        grid=(n,),
        in_specs=in_specs,
        out_specs=out_specs,
        out_shape=out_shape,
        compiler_params=_params(1),
        name="qkv_rope",
    )(*args)


def _half_masked(blk):
    lo = lax.broadcasted_iota(jnp.int32, blk.shape, 1) < HEAD_DIM
    swapped = pltpu.roll(blk, HEAD_DIM, 1)
    zero = jnp.zeros_like(blk)
    sides = ((jnp.where(lo, blk, zero), jnp.where(lo, zero, swapped)),
             (jnp.where(lo, swapped, zero), jnp.where(lo, zero, blk)))
    return [tuple(x.astype(BF16) for x in side) for side in sides]


def _softmax_block(s, sink, valid):
    if valid is not None:
        s = jnp.where(valid, s, -jnp.inf)
    m = jnp.maximum(jnp.max(s, axis=-1, keepdims=True), sink)
    p = jnp.exp(s - m)
    denom = jnp.sum(p, axis=-1, keepdims=True) + jnp.exp(sink - m)
    return p.astype(BF16), 1.0 / denom


def _attention(q_ref, q_row0, r, k_parts, v_parts, sinks_ref, valid, store):
    pairs = GROUP // 2
    nt = (((1,), (1,)), ((), ()))

    def gather(parts, m):
        return jnp.concatenate([ref[r0:r0 + n, off + m * LANES:off + (m + 1) * LANES]
                                for ref, r0, n, off in parts], axis=0)

    for m in range(KV_DIM // LANES):
        k_ops = _half_masked(gather(k_parts, m))
        v_ops = _half_masked(gather(v_parts, m))
        for gi in range(LANES // HEAD_DIM):
            p0 = (m * (LANES // HEAD_DIM) + gi) * pairs
            qg = jnp.concatenate(
                [q_ref[q_row0:q_row0 + r, (p0 + pi) * LANES:(p0 + pi + 1) * LANES]
                 for pi in range(pairs)], axis=0)
            pv, rcp = [], []
            for par in range(2):
                s = lax.dot_general(qg, k_ops[gi][par], nt, preferred_element_type=F32)
                blocks = [_softmax_block(s[pi * r:(pi + 1) * r], sinks_ref[2 * (p0 + pi) + par],
                                         valid) for pi in range(pairs)]
                p_all = jnp.concatenate([b[0] for b in blocks], axis=0)
                pv.append(jnp.dot(p_all, v_ops[gi][par], preferred_element_type=F32))
                rcp.append([b[1] for b in blocks])
            for pi in range(pairs):
                rows = slice(pi * r, (pi + 1) * r)
                o = pv[0][rows] * rcp[0][pi] + pv[1][rows] * rcp[1][pi]
                store(p0 + pi, o.astype(BF16))


def _attn_prompt_kernel(*refs, tq, convert):
    sinks_ref, x_ref, q_ref, kvp_ref, kvc_ref, wo_ref, bo_ref = refs[:7]
    if convert:
        gu32_ref, d32_ref, o_ref, gu16_ref, d16_ref, o_scr = refs[7:]
        _cast_slab(gu32_ref, d32_ref, gu16_ref, d16_ref)
    else:
        o_ref, o_scr = refs[7:]
    i = pl.program_id(0)
    sub = WINDOW
    n_prev = sub // CHUNK
    cq = lax.broadcasted_iota(jnp.int32, (sub, 2 * sub), 0) // CHUNK
    ck = lax.broadcasted_iota(jnp.int32, (sub, 2 * sub), 1) // CHUNK - n_prev
    band = (ck <= cq) & (ck >= cq - WINDOW // CHUNK)
    for t0 in range(0, tq, sub):
        if t0 == 0:
            before = kvp_ref, 0
            valid = band & ((ck >= 0) | (i > 0))
        else:
            before = kvc_ref, t0 - sub
            valid = band

        def store(pair, val, t0=t0):
            o_scr[t0:t0 + sub, pair * LANES:(pair + 1) * LANES] = val

        k_parts = [(before[0], before[1], sub, 0), (kvc_ref, t0, sub, 0)]
        v_parts = [(before[0], before[1], sub, KV_DIM), (kvc_ref, t0, sub, KV_DIM)]
        _attention(q_ref, t0, sub, k_parts, v_parts, sinks_ref, valid, store)
    y = jnp.dot(o_scr[...], wo_ref[...], preferred_element_type=F32) + bo_ref[...]
    o_ref[...] = x_ref[...] + y


DOWN_SHARE = 2


def _attn_prompt(x, q, kv, sinks, wo, bo, *, layer, tq, ffn_f32=None):
    t = x.shape[0]
    tq = min(tq, t)
    assert tq % WINDOW == 0 and t % tq == 0
    per = tq // WINDOW
    n = t // tq
    row = lambda i: (i, 0)
    const = lambda i: (0, 0)
    in_specs = [
        pl.BlockSpec(memory_space=pltpu.SMEM),
        pl.BlockSpec((tq, D_MODEL), row),
        pl.BlockSpec((tq, Q_DIM), row),
        pl.BlockSpec((WINDOW, 2 * KV_DIM), lambda i: (jnp.maximum(i * per - 1, 0), 0)),
        pl.BlockSpec((tq, 2 * KV_DIM), row),
        _resident((Q_DIM, D_MODEL), layer),
        pl.BlockSpec((1, D_MODEL), const),
    ]
    out_specs = [pl.BlockSpec((tq, D_MODEL), row)]
    out_shape = [jax.ShapeDtypeStruct((t, D_MODEL), F32)]
    args = [sinks, x, q, kv, kv, wo, bo]
    if ffn_f32 is not None:
        gu32, d32, ffn_layer = ffn_f32
        assert n % DOWN_SHARE == 0
        cast_in, cast_out, cast_shape = _cast_specs(n, n // DOWN_SHARE, ffn_layer, lambda i: i,
                                                    lambda i: i // DOWN_SHARE)
        in_specs += cast_in
        out_specs += cast_out
        out_shape += cast_shape
        args += [gu32, d32]
    return pl.pallas_call(
        functools.partial(_attn_prompt_kernel, tq=tq, convert=ffn_f32 is not None),
        grid=(n,),
        in_specs=in_specs,
        out_specs=out_specs,
        out_shape=out_shape,
        scratch_shapes=[pltpu.VMEM((tq, Q_DIM), BF16)],
        compiler_params=_params(1),
        name="attn_prompt",
    )(*args)


def _attn_sample_kernel(sinks_ref, x_ref, q_ref, kv_ref, ck_ref, cv_ref, wo_ref, bo_ref,
                        o_ref, nk_ref, nv_ref, o_scr, *, ts, wc, per):
    step = pl.program_id(0)
    for bb in range(per):
        r0 = bb * ts
        nk_ref[bb, 0:wc - ts, :] = ck_ref[bb, ts:, :]
        nk_ref[bb, wc - ts:, :] = kv_ref[r0:r0 + ts, :KV_DIM]
        nv_ref[bb, 0:wc - ts, :] = cv_ref[bb, ts:, :]
        nv_ref[bb, wc - ts:, :] = kv_ref[r0:r0 + ts, KV_DIM:]
        row0 = pl.multiple_of((step * per + bb) * ts, ts)

        def store(pair, val, row0=row0):
            o_scr[pl.ds(row0, ts), pair * LANES:(pair + 1) * LANES] = val

        k_parts = [(ck_ref.at[bb], 0, wc, 0), (kv_ref, r0, ts, 0)]
        v_parts = [(cv_ref.at[bb], 0, wc, 0), (kv_ref, r0, ts, KV_DIM)]
        _attention(q_ref, r0, ts, k_parts, v_parts, sinks_ref, None, store)

    @pl.when(step == pl.num_programs(0) - 1)
    def _():
        y = jnp.dot(o_scr[...], wo_ref[...], preferred_element_type=F32) + bo_ref[...]
        o_ref[...] = x_ref[...] + y


def _attn_sample(x, q, kv, cache_k, cache_v, sinks, wo, bo, *, layer, nb, ts, per):
    t = x.shape[0]
    wc = cache_k.shape[1]
    assert ts <= wc and ts % 16 == 0 and nb % per == 0
    row = lambda b: (b, 0)
    const = lambda b: (0, 0)
    batch = lambda b: (b, 0, 0)
    return pl.pallas_call(
        functools.partial(_attn_sample_kernel, ts=ts, wc=wc, per=per),
        grid=(nb // per,),
        in_specs=[
            pl.BlockSpec(memory_space=pltpu.SMEM),
            pl.BlockSpec((t, D_MODEL), const),
            pl.BlockSpec((per * ts, Q_DIM), row),
            pl.BlockSpec((per * ts, 2 * KV_DIM), row),
            pl.BlockSpec((per, wc, KV_DIM), batch),
            pl.BlockSpec((per, wc, KV_DIM), batch),
            _resident((Q_DIM, D_MODEL), layer),
            pl.BlockSpec((1, D_MODEL), const),
        ],
        out_specs=[pl.BlockSpec((t, D_MODEL), const),
                   pl.BlockSpec((per, wc, KV_DIM), batch),
                   pl.BlockSpec((per, wc, KV_DIM), batch)],
        out_shape=[jax.ShapeDtypeStruct((t, D_MODEL), F32),
                   jax.ShapeDtypeStruct((nb, wc, KV_DIM), F32),
                   jax.ShapeDtypeStruct((nb, wc, KV_DIM), F32)],
        scratch_shapes=[pltpu.VMEM((t, Q_DIM), BF16)],
        compiler_params=_params(1),
        name="attn_sample",
    )(sinks, x, q, kv, cache_k, cache_v, wo, bo)


def _pool_kernel(x_ref, pre_ref, g_ref, w_ref, sc_ref, o_ref, h_out_ref, ext_ref,
                 *, tm, pos0, rows_per_seq, raw_prefix):
    i = pl.program_id(0)
    x = x_ref[...]
    g = g_ref[...]
    h = _rms(x, g)
    if raw_prefix:
        pre = jnp.where(i > 0, _rms(pre_ref[...], g), 0.0)
    else:
        pre = pre_ref[...]
    ext_ref[0:POOL_PAD, :] = pre
    ext_ref[POOL_PAD:, :] = h
    h_out_ref[...] = h[tm - POOL_PAD:, :]
    row = lax.broadcasted_iota(jnp.int32, (tm, 1), 0)
    pos = pos0 + (i * tm + row) % rows_per_seq
    for gi, w in enumerate(POOL_WINDOWS):
        c0 = gi * POOL_GROUP_DIM
        sl = slice(c0, c0 + POOL_GROUP_DIM)
        acc = h[:, sl]
        for j in range(1, w):
            acc = acc + ext_ref[POOL_PAD - j:POOL_PAD - j + tm, sl]
        cnt = jnp.minimum(pos + 1, w).astype(F32)
        pooled = acc / cnt - h[:, sl]
        mixed = jnp.dot(pooled.astype(BF16), w_ref[gi], preferred_element_type=F32)
        o_ref[:, sl] = x[:, sl] + mixed * sc_ref[:, sl]


def _pool(x, prefix, g, w_group, scale, *, layer, tm, pos0, rows_per_seq, raw_prefix):
    t = x.shape[0]
    tm = min(tm, t)
    nblk = tm // POOL_PAD
    if raw_prefix:
        pre_map = lambda i: (jnp.maximum(i * nblk - 1, 0), 0)
    else:
        pre_map = lambda i: (i, 0)
    row = lambda i: (i, 0)
    const = lambda i: (0, 0)
    return pl.pallas_call(
        functools.partial(_pool_kernel, tm=tm, pos0=pos0, rows_per_seq=rows_per_seq,
                          raw_prefix=raw_prefix),
        grid=(t // tm,),
        in_specs=[
            pl.BlockSpec((tm, D_MODEL), row),
            pl.BlockSpec((POOL_PAD, D_MODEL), pre_map),
            pl.BlockSpec((1, D_MODEL), const),
            _resident((len(POOL_WINDOWS), POOL_GROUP_DIM, POOL_GROUP_DIM), layer),
            pl.BlockSpec((1, D_MODEL), const),
        ],
        out_specs=[pl.BlockSpec((tm, D_MODEL), row), pl.BlockSpec((POOL_PAD, D_MODEL), row)],
        out_shape=[jax.ShapeDtypeStruct((t, D_MODEL), F32),
                   jax.ShapeDtypeStruct((t // tm * POOL_PAD, D_MODEL), F32)],
        scratch_shapes=[pltpu.VMEM((tm + POOL_PAD, D_MODEL), F32)],
        compiler_params=_params(1),
        name="pool_mix",
    )(x, prefix, g, w_group, scale)


def _glu(x_ref, g_ref, w1_ref, b1_ref):
    h = _rms(x_ref[...], g_ref[...]).astype(BF16)
    a = jnp.dot(h, w1_ref[...], preferred_element_type=F32) + b1_ref[...]
    return a[:, :D_MODEL] * _sigmoid(a[:, D_MODEL:])


def _dwconv_rows(ext_ref, wdw_ref, c_ref, *, n_rows, ext_row0, out_row0, rc):
    def col_block(cb, carry):
        for r0 in range(0, n_rows, rc):
            acc = jnp.zeros((rc, LANES), F32)
            for j in range(CONV_WIDTH):
                start = ext_row0 + r0 + j
                acc = acc + ext_ref[cb, start:start + rc, :] * wdw_ref[cb, j:j + 1, :]
            c_ref[cb, out_row0 + r0:out_row0 + r0 + rc, :] = acc
        return carry

    lax.fori_loop(0, N_COL_BLOCKS, col_block, 0)


def _conv_tail(x_ref, c_ref, bdw_ref, lng_ref, lnb_ref, w2_ref, b2_ref, o_ref):
    c = jnp.concatenate([c_ref[cb] for cb in range(N_COL_BLOCKS)], axis=1) + bdw_ref[...]
    mu = jnp.mean(c, axis=-1, keepdims=True)
    d = c - mu
    var = jnp.mean(d * d, axis=-1, keepdims=True)
    y = d * lax.rsqrt(var + LN_EPS) * lng_ref[...] + lnb_ref[...]
    z = y * _sigmoid(y)
    out = jnp.dot(z.astype(BF16), w2_ref[...], preferred_element_type=F32) + b2_ref[...]
    o_ref[...] = x_ref[...] + out


def _conv_prompt_kernel(x_ref, g_ref, w1_ref, b1_ref, wdw_ref, bdw_ref, lng_ref, lnb_ref,
                        w2_ref, b2_ref, o_ref, st_ref, ext_ref, c_ref, *, tm):
    i = pl.program_id(0)

    @pl.when(i == 0)
    def _():
        ext_ref[:, 0:CONV_PAD, :] = jnp.zeros((N_COL_BLOCKS, CONV_PAD, LANES), F32)

    u = _glu(x_ref, g_ref, w1_ref, b1_ref)
    for cb in range(N_COL_BLOCKS):
        ext_ref[cb, CONV_PAD:, :] = u[:, cb * LANES:(cb + 1) * LANES]
    _dwconv_rows(ext_ref, wdw_ref, c_ref, n_rows=tm, ext_row0=CONV_PAD - CONV_PREFIX,
                 out_row0=0, rc=64)
    _conv_tail(x_ref, c_ref, bdw_ref, lng_ref, lnb_ref, w2_ref, b2_ref, o_ref)
    for cb in range(N_COL_BLOCKS):
        tail = ext_ref[cb, tm:tm + CONV_PAD, :]
        ext_ref[cb, 0:CONV_PAD, :] = tail
        st_ref[:, cb * LANES:(cb + 1) * LANES] = tail


def _conv_weight_specs(const, layer):
    return [
        pl.BlockSpec((1, D_MODEL), const),
        _resident((D_MODEL, 2 * D_MODEL), layer),
        pl.BlockSpec((1, 2 * D_MODEL), const),
        pl.BlockSpec((N_COL_BLOCKS, CONV_PAD, LANES), lambda i: (0, 0, 0)),
        pl.BlockSpec((1, D_MODEL), const),
        pl.BlockSpec((1, D_MODEL), const),
        pl.BlockSpec((1, D_MODEL), const),
        _resident((D_MODEL, D_MODEL), layer),
        pl.BlockSpec((1, D_MODEL), const),
    ]


def _conv_prompt(x, weights, *, layer, tm):
    t = x.shape[0]
    tm = min(tm, t)
    assert tm >= CONV_PAD
    row = lambda i: (i, 0)
    const = lambda i: (0, 0)
    return pl.pallas_call(
        functools.partial(_conv_prompt_kernel, tm=tm),
        grid=(t // tm,),
        in_specs=[pl.BlockSpec((tm, D_MODEL), row)] + _conv_weight_specs(const, layer),
        out_specs=[pl.BlockSpec((tm, D_MODEL), row), pl.BlockSpec((CONV_PAD, D_MODEL), const)],
        out_shape=[jax.ShapeDtypeStruct((t, D_MODEL), F32),
                   jax.ShapeDtypeStruct((CONV_PAD, D_MODEL), F32)],
        scratch_shapes=[pltpu.VMEM((N_COL_BLOCKS, tm + CONV_PAD, LANES), F32),
                        pltpu.VMEM((N_COL_BLOCKS, tm, LANES), F32)],
        compiler_params=_params(1),
        name="conv_prompt",
    )(x, *weights)


def _conv_sample_kernel(x_ref, st_in_ref, g_ref, w1_ref, b1_ref, wdw_ref, bdw_ref, lng_ref,
                        lnb_ref, w2_ref, b2_ref, o_ref, st_ref, ext_ref, c_ref, *, nb, ts):
    u = _glu(x_ref, g_ref, w1_ref, b1_ref)
    seg = CONV_PAD + ts
    for b in range(nb):
        for cb in range(N_COL_BLOCKS):
            cols = slice(cb * LANES, (cb + 1) * LANES)
            ext_ref[cb, b * seg:b * seg + CONV_PAD, :] = st_in_ref[b, :, cols]
            ext_ref[cb, b * seg + CONV_PAD:(b + 1) * seg, :] = u[b * ts:(b + 1) * ts, cols]
    for b in range(nb):
        _dwconv_rows(ext_ref, wdw_ref, c_ref, n_rows=ts,
                     ext_row0=b * seg + CONV_PAD - CONV_PREFIX, out_row0=b * ts, rc=ts)
    _conv_tail(x_ref, c_ref, bdw_ref, lng_ref, lnb_ref, w2_ref, b2_ref, o_ref)
    for b in range(nb):
        for cb in range(N_COL_BLOCKS):
            st_ref[b, :, cb * LANES:(cb + 1) * LANES] = ext_ref[cb, b * seg + ts:(b + 1) * seg, :]


def _conv_sample(x, state, weights, *, layer, nb, ts):
    t = x.shape[0]
    assert ts % 8 == 0
    const = lambda i: (0, 0)
    const3 = lambda i: (0, 0, 0)
    return pl.pallas_call(
        functools.partial(_conv_sample_kernel, nb=nb, ts=ts),
        grid=(1,),
        in_specs=[pl.BlockSpec((t, D_MODEL), const),
                  pl.BlockSpec((nb, CONV_PAD, D_MODEL), const3)] + _conv_weight_specs(const, layer),
        out_specs=[pl.BlockSpec((t, D_MODEL), const),
                   pl.BlockSpec((nb, CONV_PAD, D_MODEL), const3)],
        out_shape=[jax.ShapeDtypeStruct((t, D_MODEL), F32),
                   jax.ShapeDtypeStruct((nb, CONV_PAD, D_MODEL), F32)],
        scratch_shapes=[pltpu.VMEM((N_COL_BLOCKS, nb * (CONV_PAD + ts), LANES), F32),
                        pltpu.VMEM((N_COL_BLOCKS, t, LANES), F32)],
        compiler_params=_params(1),
        name="conv_sample",
    )(x, state, *weights)


def kernel(x_prompt, x_sample, cache_k, cache_v, state_pool, state_conv, norm_mix, norm_ffn, norm_final, a_w_qkv, a_b_qkv, a_sinks, a_w_o, a_b_o, b_w_group, b_scale, c_w_pw1, c_b_pw1, c_w_dw, c_b_dw, c_ln_g, c_ln_b, c_w_pw2, c_b_pw2, f_w_gate_up, f_w_down):
    bp, seq, _ = x_prompt.shape
    nb, ts, _ = x_sample.shape
    assert bp == 1 and seq % CHUNK == 0
    wc = cache_k.shape[2]
    xp = x_prompt.reshape(seq, D_MODEL)
    xs = x_sample.reshape(nb * ts, D_MODEL)
    row = lambda a: a.reshape(1, -1)

    tables_p = _rope_tables(jnp.arange(seq))
    tables_s = tuple(jnp.tile(a, (nb, 1)) for a in _rope_tables(PAST_LEN + jnp.arange(ts)))
    g_final = row(norm_final)

    w_qkv, w_o = a_w_qkv.astype(BF16), a_w_o.astype(BF16)
    w_g = b_w_group.astype(BF16)
    w_pw1, w_pw2 = c_w_pw1.astype(BF16), c_w_pw2.astype(BF16)
    ffn_w = {}
    stacked = (f_w_gate_up, f_w_down)

    kp_l, vp_l, ks_l, vs_l = [], [], [], []
    pp_l, ps_l, cp_l, cs_l = [], [], [], []
    for i in range(DEPTH):
        kind, j = i % N_MIXERS, i // N_MIXERS
        g_mix = row(norm_mix[i])
        if kind == 0:
            b_qkv, b_o = row(a_b_qkv[j]), row(a_b_o[j])
            q, kv, w_gu, w_d = _qkv(xp, g_mix, w_qkv, b_qkv, tables_p, layer=j, tm=512,
                                    ffn_f32=stacked + (i,))
            ffn_w[i] = (w_gu, w_d)
            if i + 1 < DEPTH and (i + 1) % N_MIXERS != 0:
                xp, w_gu, w_d = _attn_prompt(xp, q, kv, a_sinks[j], w_o, b_o, layer=j, tq=256,
                                             ffn_f32=stacked + (i + 1,))
                ffn_w[i + 1] = (w_gu, w_d)
            else:
                xp = _attn_prompt(xp, q, kv, a_sinks[j], w_o, b_o, layer=j, tq=256)[0]
            keep = min(WINDOW, seq)
            kp_l.append(kv[seq - keep:, :KV_DIM].reshape(1, keep, N_KV_HEADS, HEAD_DIM))
            vp_l.append(kv[seq - keep:, KV_DIM:].reshape(1, keep, N_KV_HEADS, HEAD_DIM))
            q, kv = _qkv(xs, g_mix, w_qkv, b_qkv, tables_s, layer=j, tm=nb * ts)
            xs, nk, nv = _attn_sample(xs, q, kv, cache_k[j].reshape(nb, wc, KV_DIM),
                                      cache_v[j].reshape(nb, wc, KV_DIM), a_sinks[j], w_o, b_o,
                                      layer=j, nb=nb, ts=ts, per=4)
            ks_l.append(nk.reshape(nb, wc, N_KV_HEADS, HEAD_DIM))
            vs_l.append(nv.reshape(nb, wc, N_KV_HEADS, HEAD_DIM))
        elif kind == 1:
            sc = row(b_scale[j])
            xp, hp = _pool(xp, xp, g_mix, w_g, sc, layer=j, tm=256, pos0=0, rows_per_seq=seq,
                           raw_prefix=True)
            pp_l.append(hp[-POOL_PREFIX:].reshape(1, POOL_PREFIX, D_MODEL))
            assert ts == POOL_PAD
            pre = jnp.pad(state_pool[j], ((0, 0), (POOL_PAD - POOL_PREFIX, 0), (0, 0)))
            xs, hs = _pool(xs, pre.reshape(nb * POOL_PAD, D_MODEL), g_mix, w_g, sc, layer=j,
                           tm=ts, pos0=PAST_LEN, rows_per_seq=ts, raw_prefix=False)
            ext = jnp.concatenate([state_pool[j], hs.reshape(nb, ts, D_MODEL)], axis=1)
            ps_l.append(ext[:, -POOL_PREFIX:])
        else:
            w_dw = jnp.pad(c_w_dw[j], ((0, CONV_PAD - CONV_WIDTH), (0, 0)))
            w_dw = w_dw.reshape(CONV_PAD, N_COL_BLOCKS, LANES).transpose(1, 0, 2)
            weights = (g_mix, w_pw1, row(c_b_pw1[j]), w_dw, row(c_b_dw[j]),
                       row(c_ln_g[j]), row(c_ln_b[j]), w_pw2, row(c_b_pw2[j]))
            xp, st = _conv_prompt(xp, weights, layer=j, tm=256)
            cp_l.append(st[-CONV_PREFIX:].reshape(1, CONV_PREFIX, D_MODEL))
            pre = jnp.pad(state_conv[j], ((0, 0), (CONV_PAD - CONV_PREFIX, 0), (0, 0)))
            xs, st = _conv_sample(xs, pre, weights, layer=j, nb=nb, ts=ts)
            cs_l.append(st[:, -CONV_PREFIX:])
        last = i == DEPTH - 1
        g_ffn = row(norm_ffn[i])
        w_gu, w_d = ffn_w[i]
        xs = _ffn(xs, g_ffn, w_gu, w_d, g_final, final_norm=last, tm=256)[0]
        if last or i + 1 in ffn_w or (i + 1) % N_MIXERS == 0:
            xp = _ffn(xp, g_ffn, w_gu, w_d, g_final, final_norm=last, tm=512)[0]
        else:
            xp, w_gu, w_d = _ffn(xp, g_ffn, w_gu, w_d, g_final, final_norm=False, tm=512,
                                 next_f32=stacked + (i + 1,))
            ffn_w[i + 1] = (w_gu, w_d)
    return (xp.reshape(1, seq, D_MODEL), xs.reshape(nb, ts, D_MODEL),
            jnp.stack(kp_l), jnp.stack(vp_l), jnp.stack(ks_l), jnp.stack(vs_l),
            jnp.stack(pp_l), jnp.stack(ps_l), jnp.stack(cp_l), jnp.stack(cs_l))
```
